```python
import jax, jax.numpy as jnp
from jax import lax
import numpy as np

D_MODEL = 1024
BATCH = 4
SEQ = 4096
DEPTH = 4
DEC_BATCH = 128
DEC_SEQ = 4
PAST_LEN = 2048
PAGE_SIZE = 128

N_META = 16
SB_HEADS = 8
SB_HEAD_DIM = 64
SB_WIDTH = SB_HEADS * SB_HEAD_DIM
CONV_WIDTH = D_MODEL - SB_WIDTH
MIX_WIDTH = SB_WIDTH + CONV_WIDTH
IN_WIDTH = 3 * SB_WIDTH + 2 * CONV_WIDTH
CONV_KERNEL = 31
CONV_STATE = CONV_KERNEL - 1
D_FF = 2816
N_EXPERTS = 8
TOP_K = 2
Q_BLOCK = 128
SB_BIAS_INIT = -6.0
RMS_EPS = 1e-6
LN_EPS = 1e-5

kernel_name = "hymba_stickbreak_conformer_step"


def rms_norm(x, g):
    xf = x.astype(jnp.float32)
    y = xf * lax.rsqrt(jnp.mean(xf * xf, axis=-1, keepdims=True) + RMS_EPS)
    return (y * g.astype(jnp.float32)).astype(x.dtype)


def layer_norm(x, g, b):
    xf = x.astype(jnp.float32)
    mu = jnp.mean(xf, axis=-1, keepdims=True)
    var = jnp.mean(jnp.square(xf - mu), axis=-1, keepdims=True)
    y = (xf - mu) * lax.rsqrt(var + LN_EPS) * g.astype(jnp.float32) + b.astype(jnp.float32)
    return y.astype(x.dtype)


def stick_breaking(q, k, v, bias, valid):
    z = jnp.einsum("bqhd,bkhd->bhqk", q, k, preferred_element_type=jnp.float32) * (SB_HEAD_DIM ** -0.5)
    z = z + bias.astype(jnp.float32)[None, :, None, None]
    log_beta = jax.nn.log_sigmoid(z)
    log_one_minus = jnp.where(valid, jax.nn.log_sigmoid(-z), 0.0)
    log_remain = lax.cumsum(log_one_minus, axis=3, reverse=True) - log_one_minus
    a = jnp.where(valid, jnp.exp(log_beta + log_remain), 0.0)
    out = jnp.einsum("bhqk,bkhd->bqhd", a.astype(v.dtype), v, preferred_element_type=jnp.float32)
    return out.astype(q.dtype)


def sb_attention_prompt(q, k, v, bias):
    b, n, h, dh = q.shape
    pad = (-n) % Q_BLOCK
    length = n + pad
    padf = lambda a: jnp.pad(a, ((0, 0), (pad, 0), (0, 0), (0, 0)))
    qp, kp, vp = padf(q), padf(k), padf(v)
    n_blocks = length // Q_BLOCK
    q_blocks = qp.reshape(b, n_blocks, Q_BLOCK, h, dh).transpose(1, 0, 2, 3, 4)
    key_idx = jnp.arange(length)

    def one_block(args):
        i, qb = args
        q_idx = i * Q_BLOCK + jnp.arange(Q_BLOCK)
        valid = (key_idx[None, :] < q_idx[:, None]) & (key_idx[None, :] >= pad)
        return stick_breaking(qb, kp, vp, bias, valid)

    out = lax.map(one_block, (jnp.arange(n_blocks), q_blocks))
    out = out.transpose(1, 0, 2, 3, 4).reshape(b, length, h, dh)
    return out[:, pad:]


def in_projection(h, w_in):
    proj = jnp.einsum("btd,de->bte", h, w_in)
    q, k, v, a, g = jnp.split(proj, [SB_WIDTH, 2 * SB_WIDTH, 3 * SB_WIDTH, 3 * SB_WIDTH + CONV_WIDTH], axis=-1)
    heads = lambda t: t.reshape(t.shape[0], t.shape[1], SB_HEADS, SB_HEAD_DIM)
    glu = a * jax.nn.sigmoid(g)
    return heads(q), heads(k), heads(v), glu


def conv_branch(glu, prefix, conv_w, conv_b, ln_g, ln_b):
    ext = jnp.concatenate([prefix, glu], axis=1)
    y = lax.conv_general_dilated(ext, conv_w[:, None, :].astype(ext.dtype), window_strides=(1,), padding="VALID",
                                 dimension_numbers=("NWC", "WIO", "NWC"), feature_group_count=CONV_WIDTH)
    y = y + conv_b
    y = jax.nn.silu(layer_norm(y, ln_g, ln_b))
    return y, ext[:, -CONV_STATE:]


def merge_groups(attn, conv, g_attn, g_conv, w_out):
    b, t = attn.shape[:2]
    cat = jnp.concatenate([rms_norm(attn.reshape(b, t, SB_WIDTH), g_attn), rms_norm(conv, g_conv)], axis=-1)
    return jnp.einsum("bte,ed->btd", cat, w_out)


def swiglu(h, wg, wu, wd):
    hidden = jax.nn.silu(jnp.einsum("btd,df->btf", h, wg)) * jnp.einsum("btd,df->btf", h, wu)
    return jnp.einsum("btf,fd->btd", hidden, wd)


def moe_swiglu(h, w_router, wg, wu, wd):
    logits = jnp.einsum("btd,de->bte", h, w_router, preferred_element_type=jnp.float32)
    top_vals, top_idx = lax.top_k(logits, TOP_K)
    top_w = jax.nn.softmax(top_vals, axis=-1)
    gates = jnp.sum(jax.nn.one_hot(top_idx, N_EXPERTS, dtype=jnp.float32) * top_w[..., None], axis=-2)
    out = jnp.zeros(h.shape, jnp.float32)
    for e in range(N_EXPERTS):
        out = out + gates[..., e:e + 1] * swiglu(h, wg[e], wu[e], wd[e])
    return out.astype(h.dtype)


def setup_inputs(seed: int = 0) -> dict:
    key = jax.random.key(seed)
    ks = jax.random.split(key, 27)
    n_pages = PAST_LEN // PAGE_SIZE
    n_phys = (5 * DEC_BATCH * n_pages) // 4
    n_dense = (DEPTH + 1) // 2
    n_moe = DEPTH // 2

    def nrm(k, shape, scale=1.0):
        return jax.random.normal(k, shape, jnp.float32) * scale

    def gain(k, shape):
        return 1.0 + 0.02 * jax.random.normal(k, shape, jnp.float32)

    perm = jax.random.permutation(ks[3], n_phys)
    page_table = perm[: DEC_BATCH * n_pages].reshape(DEC_BATCH, n_pages).astype(jnp.int32)
    return {
        "x_prompt": nrm(ks[0], (BATCH, SEQ, D_MODEL)),
        "x_sample": nrm(ks[1], (DEC_BATCH, DEC_SEQ, D_MODEL)),
        "cache_k": nrm(ks[2], (DEPTH, n_phys, PAGE_SIZE, SB_HEADS, SB_HEAD_DIM)),
        "cache_v": nrm(ks[4], (DEPTH, n_phys, PAGE_SIZE, SB_HEADS, SB_HEAD_DIM)),
        "state_conv": nrm(ks[5], (DEPTH, DEC_BATCH, CONV_STATE, CONV_WIDTH)),
        "page_table": page_table,
        "meta_tokens": nrm(ks[6], (N_META, D_MODEL)),
        "g_mix": gain(ks[7], (DEPTH, D_MODEL)),
        "w_in": nrm(ks[8], (DEPTH, D_MODEL, IN_WIDTH), D_MODEL ** -0.5),
        "sb_bias": SB_BIAS_INIT + nrm(ks[25], (DEPTH, SB_HEADS), 0.5),
        "conv_w": nrm(ks[9], (DEPTH, CONV_KERNEL, CONV_WIDTH), CONV_KERNEL ** -0.5),
        "conv_b": nrm(ks[10], (DEPTH, CONV_WIDTH), 0.02),
        "conv_ln_g": gain(ks[11], (DEPTH, CONV_WIDTH)),
        "conv_ln_b": nrm(ks[12], (DEPTH, CONV_WIDTH), 0.02),
        "g_out_attn": gain(ks[13], (DEPTH, SB_WIDTH)),
        "g_out_conv": gain(ks[14], (DEPTH, CONV_WIDTH)),
        "w_out": nrm(ks[15], (DEPTH, MIX_WIDTH, D_MODEL), MIX_WIDTH ** -0.5),
        "g_ffn": gain(ks[16], (DEPTH, D_MODEL)),
        "w_gate_dense": nrm(ks[17], (n_dense, D_MODEL, D_FF), D_MODEL ** -0.5),
        "w_up_dense": nrm(ks[18], (n_dense, D_MODEL, D_FF), D_MODEL ** -0.5),
        "w_down_dense": nrm(ks[19], (n_dense, D_FF, D_MODEL), D_FF ** -0.5),
        "w_router": nrm(ks[20], (n_moe, D_MODEL, N_EXPERTS), D_MODEL ** -0.5),
        "w_gate_moe": nrm(ks[21], (n_moe, N_EXPERTS, D_MODEL, D_FF), D_MODEL ** -0.5),
        "w_up_moe": nrm(ks[22], (n_moe, N_EXPERTS, D_MODEL, D_FF), D_MODEL ** -0.5),
        "w_down_moe": nrm(ks[23], (n_moe, N_EXPERTS, D_FF, D_MODEL), D_FF ** -0.5),
        "g_final": gain(ks[24], (D_MODEL,)),
    }


def reference(x_prompt, x_sample, cache_k, cache_v, state_conv, page_table, meta_tokens, g_mix, w_in, sb_bias,
              conv_w, conv_b, conv_ln_g, conv_ln_b, g_out_attn, g_out_conv, w_out, g_ffn,
              w_gate_dense, w_up_dense, w_down_dense, w_router, w_gate_moe, w_up_moe, w_down_moe, g_final):
    b = x_prompt.shape[0]
    bd, t_new = x_sample.shape[0], x_sample.shape[1]
    past = page_table.shape[1] * cache_k.shape[2]
    xp = jnp.concatenate([jnp.broadcast_to(meta_tokens[None].astype(x_prompt.dtype), (b, N_META, D_MODEL)), x_prompt], axis=1)
    xs = x_sample
    key_pos = jnp.arange(past + t_new)
    valid_s = key_pos[None, :] < (past + jnp.arange(t_new))[:, None]
    conv_zero = jnp.zeros((b, CONV_STATE, CONV_WIDTH), xp.dtype)

    kp_list, vp_list, ks_list, vs_list, cp_list, cs_list = [], [], [], [], [], []
    for l in range(DEPTH):
        hp = rms_norm(xp, g_mix[l])
        hs = rms_norm(xs, g_mix[l])
        qp, kp, vp, glu_p = in_projection(hp, w_in[l])
        qs, ks_, vs_, glu_s = in_projection(hs, w_in[l])

        attn_p = sb_attention_prompt(qp, kp, vp, sb_bias[l])
        k_past = cache_k[l][page_table].reshape(bd, past, SB_HEADS, SB_HEAD_DIM).astype(ks_.dtype)
        v_past = cache_v[l][page_table].reshape(bd, past, SB_HEADS, SB_HEAD_DIM).astype(vs_.dtype)
        attn_s = stick_breaking(qs, jnp.concatenate([k_past, ks_], axis=1),
                                jnp.concatenate([v_past, vs_], axis=1), sb_bias[l], valid_s)

        conv_p, conv_state_p = conv_branch(glu_p, conv_zero, conv_w[l], conv_b[l], conv_ln_g[l], conv_ln_b[l])
        conv_s, conv_state_s = conv_branch(glu_s, state_conv[l].astype(glu_s.dtype), conv_w[l], conv_b[l],
                                           conv_ln_g[l], conv_ln_b[l])

        xp = xp + merge_groups(attn_p, conv_p, g_out_attn[l], g_out_conv[l], w_out[l])
        xs = xs + merge_groups(attn_s, conv_s, g_out_attn[l], g_out_conv[l], w_out[l])

        hp = rms_norm(xp, g_ffn[l])
        hs = rms_norm(xs, g_ffn[l])
        j = l // 2
        if l % 2 == 0:
            xp = xp + swiglu(hp, w_gate_dense[j], w_up_dense[j], w_down_dense[j])
            xs = xs + swiglu(hs, w_gate_dense[j], w_up_dense[j], w_down_dense[j])
        else:
            xp = xp + moe_swiglu(hp, w_router[j], w_gate_moe[j], w_up_moe[j], w_down_moe[j])
            xs = xs + moe_swiglu(hs, w_router[j], w_gate_moe[j], w_up_moe[j], w_down_moe[j])

        kp_list.append(kp); vp_list.append(vp)
        ks_list.append(ks_); vs_list.append(vs_)
        cp_list.append(conv_state_p); cs_list.append(conv_state_s)

    y_prompt = rms_norm(xp, g_final)[:, N_META:]
    y_sample = rms_norm(xs, g_final)
    k_prompt_new = jnp.stack(kp_list)
    v_prompt_new = jnp.stack(vp_list)
    k_sample_new = jnp.stack(ks_list)
    v_sample_new = jnp.stack(vs_list)
    conv_prompt_new = jnp.stack(cp_list)
    conv_sample_new = jnp.stack(cs_list)
    return (y_prompt, y_sample, k_prompt_new, v_prompt_new, k_sample_new, v_sample_new, conv_prompt_new, conv_sample_new)
```

```python
import functools

import jax
import jax.numpy as jnp
from jax import lax
from jax.experimental import pallas as pl
from jax.experimental.pallas import tpu as pltpu

F32 = jnp.float32
BF16 = jnp.bfloat16

N_META = 16
SB_HEADS = 8
SB_HEAD_DIM = 64
SB_WIDTH = SB_HEADS * SB_HEAD_DIM
CONV_KERNEL = 31
CONV_STATE = CONV_KERNEL - 1
N_EXPERTS = 8
TOP_K = 2
RMS_EPS = 1e-6
LN_EPS = 1e-5

LANES = 128
ATT_BLOCK = 128
CONV_HALO = 32
VMEM_LIMIT = 56 * 1024 * 1024


def _cparams(sem):
    return pltpu.CompilerParams(dimension_semantics=sem, vmem_limit_bytes=VMEM_LIMIT)


def _rms(x, g):
    return x * lax.rsqrt(jnp.mean(x * x, axis=-1, keepdims=True) + RMS_EPS) * g


def _const_spec(shape):
    nd = len(shape)
    return pl.BlockSpec(shape, lambda *_: (0,) * nd)


def _in_proj_kernel(x_ref, g_ref, w_ref, qb_ref, kb_ref, vb_ref, kf_ref, vf_ref, glu_ref,
                    *, tm, n_batch, seq_pad, n_padrows):
    i = pl.program_id(0)
    h = _rms(x_ref[...], g_ref[...])
    r = i * tm + lax.broadcasted_iota(jnp.int32, (tm, 1), 0)
    rb = r
    for b in range(1, n_batch):
        rb = jnp.where(r >= b * seq_pad, r - b * seq_pad, rb)
    valid = (r >= n_batch * seq_pad) | (rb >= n_padrows)
    h = jnp.where(valid, h, 0.0).astype(BF16)
    w = SB_WIDTH
    q = jnp.dot(h, w_ref[:, 0:w], preferred_element_type=F32)
    qb_ref[...] = (q * (SB_HEAD_DIM ** -0.5)).astype(BF16)
    k = jnp.dot(h, w_ref[:, w:2 * w], preferred_element_type=F32)
    kf_ref[...] = k
    kb_ref[...] = k.astype(BF16)
    v = jnp.dot(h, w_ref[:, 2 * w:3 * w], preferred_element_type=F32)
    vf_ref[...] = v
    vb_ref[...] = v.astype(BF16)
    cw = (w_ref.shape[1] - 3 * w) // 2
    a = jnp.dot(h, w_ref[:, 3 * w:3 * w + cw], preferred_element_type=F32)
    g = jnp.dot(h, w_ref[:, 3 * w + cw:3 * w + 2 * cw], preferred_element_type=F32)
    glu_ref[...] = a * jax.nn.sigmoid(g)


def _in_proj(x, g, w_bf, *, tm, n_batch, seq_pad, n_padrows):
    nt, d = x.shape
    e = w_bf.shape[1]
    cw = (e - 3 * SB_WIDTH) // 2
    row = lambda width: pl.BlockSpec((tm, width), lambda i: (i, 0))
    return pl.pallas_call(
        functools.partial(_in_proj_kernel, tm=tm, n_batch=n_batch, seq_pad=seq_pad, n_padrows=n_padrows),
        grid=(nt // tm,),
        in_specs=[row(d), _const_spec((1, d)), _const_spec((d, e))],
        out_specs=[row(SB_WIDTH)] * 5 + [row(cw)],
        out_shape=[jax.ShapeDtypeStruct((nt, SB_WIDTH), BF16)] * 3
        + [jax.ShapeDtypeStruct((nt, SB_WIDTH), F32)] * 2 + [jax.ShapeDtypeStruct((nt, cw), F32)],
        compiler_params=_cparams(("parallel",)),
        name="in_proj",
    )(x, g.reshape(1, d), w_bf)


def _softplus(z):
    return jnp.maximum(z, 0.0) + jnp.log(1.0 + jnp.exp(-jnp.abs(z)))


def _split_bf16(x):
    hi = x.astype(BF16)
    lo = (x - hi.astype(F32)).astype(BF16)
    return hi, lo


def _suffix_tri(n):
    j = lax.broadcasted_iota(jnp.int32, (2 * n, 2 * n), 0) % n
    s = lax.broadcasted_iota(jnp.int32, (2 * n, 2 * n), 1)
    return jnp.where((s >= n) | (j >= s), 1.0, 0.0).astype(BF16)


def _sb_prompt_kernel(bias_ref, q_ref, k_ref, v_ref, tri_ref, o_ref, carry_ref, acc_ref):
    hp = pl.program_id(1)
    qb = pl.program_id(2)
    n = ATT_BLOCK
    lane = lax.broadcasted_iota(jnp.int32, (1, LANES), 1)
    row = lax.broadcasted_iota(jnp.int32, (n, n), 0)
    col = lax.broadcasted_iota(jnp.int32, (n, n), 1)
    below_diag = col < row
    q = q_ref[...]
    for hh in range(2):
        in_head = (lane >= SB_HEAD_DIM) == bool(hh)
        qh = jnp.where(in_head, q, jnp.zeros_like(q))
        bias = bias_ref[hp * 2 + hh]
        carry_ref[...] = jnp.zeros_like(carry_ref)
        acc_ref[hh] = jnp.zeros((n, LANES), F32)

        def block(j, diag, qh=qh, bias=bias, hh=hh):
            start = pl.multiple_of(j * n, n)
            kj = k_ref[pl.ds(start, n), :]
            vj = v_ref[pl.ds(start, n), :]
            z = lax.dot_general(qh, kj, (((1,), (1,)), ((), ())), preferred_element_type=F32) + bias
            sp = _softplus(z)
            if diag:
                sp = jnp.where(below_diag, sp, 0.0)
            hi, lo = _split_bf16(sp)
            cs2 = jnp.dot(jnp.concatenate([hi, lo], axis=1), tri_ref[...], preferred_element_type=F32)
            a = jnp.exp(z - cs2[:, :n] - carry_ref[...])
            if diag:
                a = jnp.where(below_diag, a, 0.0)
            acc_ref[hh] += jnp.dot(a.astype(BF16), vj, preferred_element_type=F32)
            carry_ref[...] += cs2[:, n:]

        block(qb, True)

        def body(t, c):
            block(qb - 1 - t, False)
            return c

        lax.fori_loop(0, qb, body, 0)
    o_ref[...] = jnp.where(lane < SB_HEAD_DIM, acc_ref[0], acc_ref[1])


def _sb_prompt(q_bf, k_bf, v_bf, bias, tri, *, n_batch, seq_pad):
    n = ATT_BLOCK
    nqb = seq_pad // n
    n_hp = SB_WIDTH // LANES
    return pl.pallas_call(
        _sb_prompt_kernel,
        grid_spec=pltpu.PrefetchScalarGridSpec(
            num_scalar_prefetch=1,
            grid=(n_batch, n_hp, nqb),
            in_specs=[
                pl.BlockSpec((n, LANES), lambda b, h, i, *_: (b * nqb + i, h)),
                pl.BlockSpec((seq_pad, LANES), lambda b, h, i, *_: (b, h)),
                pl.BlockSpec((seq_pad, LANES), lambda b, h, i, *_: (b, h)),
                pl.BlockSpec((2 * n, 2 * n), lambda b, h, i, *_: (0, 0)),
            ],
            out_specs=pl.BlockSpec((n, LANES), lambda b, h, i, *_: (b * nqb + i, h)),
            scratch_shapes=[pltpu.VMEM((n, n), F32), pltpu.VMEM((2, n, LANES), F32)],
        ),
        out_shape=jax.ShapeDtypeStruct((n_batch * seq_pad, SB_WIDTH), F32),
        compiler_params=_cparams(("parallel", "parallel", "arbitrary")),
        name="sb_prompt",
    )(bias, q_bf, k_bf, v_bf, tri)


def _sb_sample_kernel(pt_ref, bias_ref, q_ref, kn_ref, vn_ref, kc_ref, vc_ref, tri_ref, o_ref,
                      qrows_ref, kblk_ref, vblk_ref, carry_ref, acc_ref, *, t_new, n_pages):
    del pt_ref
    j = pl.program_id(1)
    n = ATT_BLOCK
    nc = t_new * SB_HEADS
    key = lax.broadcasted_iota(jnp.int32, (n, nc), 0)
    c_t = lax.broadcasted_iota(jnp.int32, (n, nc), 1) // SB_HEADS
    new_valid = key < c_t

    def block(kb, vb, masked):
        z = lax.dot_general(kb, qrows_ref[...], (((1,), (1,)), ((), ())), preferred_element_type=F32)
        z = z + bias_ref[...]
        sp = _softplus(z)
        if masked:
            sp = jnp.where(new_valid, sp, 0.0)
        hi, lo = _split_bf16(sp)
        cs = jnp.dot(tri_ref[...], jnp.concatenate([hi, lo], axis=0), preferred_element_type=F32)
        a = jnp.exp(z - cs - carry_ref[...])
        if masked:
            a = jnp.where(new_valid, a, 0.0)
        acc_ref[...] += lax.dot_general(a.astype(BF16), vb, (((0,), (0,)), ((), ())), preferred_element_type=F32)
        carry_ref[...] += jnp.sum(sp, axis=0, keepdims=True)

    @pl.when(j == 0)
    def _():
        head_of_lane = lax.broadcasted_iota(jnp.int32, (SB_HEADS, SB_WIDTH), 1) // SB_HEAD_DIM
        head_of_row = lax.broadcasted_iota(jnp.int32, (SB_HEADS, SB_WIDTH), 0)
        q = q_ref[0].astype(F32)
        rows = []
        for t in range(t_new):
            qt = jnp.broadcast_to(q[t:t + 1, :], (SB_HEADS, SB_WIDTH))
            rows.append(jnp.where(head_of_lane == head_of_row, qt, 0.0))
        qrows_ref[...] = jnp.concatenate(rows, axis=0).astype(BF16)
        carry_ref[...] = jnp.zeros_like(carry_ref)
        acc_ref[...] = jnp.zeros_like(acc_ref)
        kblk_ref[...] = jnp.zeros_like(kblk_ref)
        vblk_ref[...] = jnp.zeros_like(vblk_ref)
        kblk_ref[0:t_new, :] = kn_ref[0]
        vblk_ref[0:t_new, :] = vn_ref[0]
        block(kblk_ref[...].astype(BF16), vblk_ref[...].astype(BF16), True)

    block(kc_ref[...].astype(BF16), vc_ref[...].astype(BF16), False)

    @pl.when(j == n_pages - 1)
    def _():
        head_of_lane = lax.broadcasted_iota(jnp.int32, (SB_HEADS, SB_WIDTH), 1) // SB_HEAD_DIM
        head_of_row = lax.broadcasted_iota(jnp.int32, (SB_HEADS, SB_WIDTH), 0)
        for t in range(t_new):
            blk = acc_ref[t * SB_HEADS:(t + 1) * SB_HEADS, :]
            o_ref[0, t:t + 1, :] = jnp.sum(jnp.where(head_of_lane == head_of_row, blk, 0.0), axis=0, keepdims=True)


def _sb_sample(page_table, q_bf, k_new, v_new, cache_k, cache_v, bias_cols, tri_t, layer):
    bd, t_new, w = q_bf.shape
    n_pages = page_table.shape[1]
    page = cache_k.shape[2]
    assert page == ATT_BLOCK
    nc = t_new * SB_HEADS
    seq = lambda: pl.BlockSpec((1, t_new, w), lambda b, j, pt: (b, 0, 0))
    cache = lambda: pl.BlockSpec((None, None, page, w), lambda b, j, pt: (layer, pt[b, n_pages - 1 - j], 0, 0))
    return pl.pallas_call(
        functools.partial(_sb_sample_kernel, t_new=t_new, n_pages=n_pages),
        grid_spec=pltpu.PrefetchScalarGridSpec(
            num_scalar_prefetch=1,
            grid=(bd, n_pages),
            in_specs=[
                pl.BlockSpec((1, nc), lambda b, j, pt: (0, 0)),
                seq(), seq(), seq(), cache(), cache(),
                pl.BlockSpec((page, 2 * page), lambda b, j, pt: (0, 0)),
            ],
            out_specs=seq(),
            scratch_shapes=[
                pltpu.VMEM((nc, w), BF16),
                pltpu.VMEM((page, w), F32),
                pltpu.VMEM((page, w), F32),
                pltpu.VMEM((1, nc), F32),
                pltpu.VMEM((nc, w), F32),
            ],
        ),
        out_shape=jax.ShapeDtypeStruct((bd, t_new, w), F32),
        compiler_params=_cparams(("parallel", "arbitrary")),
        name="sb_sample",
    )(page_table, bias_cols, q_bf, k_new, v_new, cache_k, cache_v, tri_t)


def _ln_silu(y, g, b):
    mu = jnp.mean(y, axis=-1, keepdims=True)
    d = y - mu
    var = jnp.mean(d * d, axis=-1, keepdims=True)
    y = d * lax.rsqrt(var + LN_EPS) * g + b
    return y * jax.nn.sigmoid(y)


def _conv_prompt_kernel(glu_ref, w_ref, cb_ref, lg_ref, lb_ref, o_ref, win_ref, *, tc):
    i = pl.program_id(1)
    halo = CONV_HALO

    @pl.when(i == 0)
    def _():
        win_ref[0:halo, :] = jnp.zeros((halo, win_ref.shape[1]), F32)

    win_ref[halo:halo + tc, :] = glu_ref[...]
    acc = jnp.zeros((tc, win_ref.shape[1]), F32) + cb_ref[...]
    for j in range(CONV_KERNEL):
        off = halo - CONV_STATE + j
        acc = acc + win_ref[off:off + tc, :] * w_ref[j:j + 1, :]
    o_ref[...] = _ln_silu(acc, lg_ref[...], lb_ref[...])
    win_ref[0:halo, :] = win_ref[tc:tc + halo, :]


def _conv_prompt(glu, conv_w, conv_b, ln_g, ln_b, *, n_batch, seq_pad, tc):
    cw = glu.shape[1]
    nb = seq_pad // tc
    vec = lambda: _const_spec((1, cw))
    return pl.pallas_call(
        functools.partial(_conv_prompt_kernel, tc=tc),
        grid=(n_batch, nb),
        in_specs=[pl.BlockSpec((tc, cw), lambda b, i: (b * nb + i, 0)), _const_spec((CONV_KERNEL, cw)), vec(), vec(), vec()],
        out_specs=pl.BlockSpec((tc, cw), lambda b, i: (b * nb + i, 0)),
        out_shape=jax.ShapeDtypeStruct((n_batch * seq_pad, cw), F32),
        scratch_shapes=[pltpu.VMEM((CONV_HALO + tc, cw), F32)],
        compiler_params=_cparams(("parallel", "arbitrary")),
        name="conv_prompt",
    )(glu, conv_w, conv_b.reshape(1, cw), ln_g.reshape(1, cw), ln_b.reshape(1, cw))


def _conv_sample_kernel(st_ref, glu_ref, w_ref, cb_ref, lg_ref, lb_ref, o_ref, *, t_new):
    for t in range(t_new):
        acc = jnp.zeros(o_ref.shape[1:], F32) + cb_ref[...]
        for j in range(CONV_KERNEL):
            r = t + j
            tap = st_ref[r] if r < CONV_STATE else glu_ref[r - CONV_STATE]
            acc = acc + tap * w_ref[j:j + 1, :]
        o_ref[t] = _ln_silu(acc, lg_ref[...], lb_ref[...])


def _conv_sample(state_t, glu_t, conv_w, conv_b, ln_g, ln_b):
    t_new, bd, cw = glu_t.shape
    vec = lambda: _const_spec((1, cw))
    return pl.pallas_call(
        functools.partial(_conv_sample_kernel, t_new=t_new),
        grid=(1,),
        in_specs=[_const_spec(state_t.shape), _const_spec(glu_t.shape), _const_spec((CONV_KERNEL, cw)), vec(), vec(), vec()],
        out_specs=_const_spec((t_new, bd, cw)),
        out_shape=jax.ShapeDtypeStruct((t_new, bd, cw), F32),
        compiler_params=_cparams(("arbitrary",)),
        name="conv_sample",
    )(state_t, glu_t, conv_w, conv_b.reshape(1, cw), ln_g.reshape(1, cw), ln_b.reshape(1, cw))


def _merge_kernel(x_ref, a_ref, c_ref, ga_ref, gc_ref, wa_ref, wc_ref, o_ref):
    na = _rms(a_ref[...], ga_ref[...]).astype(BF16)
    nc = _rms(c_ref[...], gc_ref[...]).astype(BF16)
    y = jnp.dot(na, wa_ref[...], preferred_element_type=F32)
    y = y + jnp.dot(nc, wc_ref[...], preferred_element_type=F32)
    o_ref[...] = x_ref[...] + y


def _merge(x, attn, conv, g_a, g_c, w_out_bf, *, tm):
    nt, d = x.shape
    wa = attn.shape[1]
    wc = conv.shape[1]
    row = lambda width: pl.BlockSpec((tm, width), lambda i: (i, 0))
    return pl.pallas_call(
        _merge_kernel,
        grid=(nt // tm,),
        in_specs=[row(d), row(wa), row(wc), _const_spec((1, wa)), _const_spec((1, wc)),
                  pl.BlockSpec((wa, d), lambda i: (0, 0)), pl.BlockSpec((wc, d), lambda i: (wa // wc, 0))],
        out_specs=row(d),
        out_shape=jax.ShapeDtypeStruct((nt, d), F32),
        compiler_params=_cparams(("parallel",)),
        name="merge",
    )(x, attn, conv, g_a.reshape(1, wa), g_c.reshape(1, wc), w_out_bf, w_out_bf)


def _swiglu_core(h_bf, wg_ref, wu_ref, wd_ref):
    g = jnp.dot(h_bf, wg_ref[...], preferred_element_type=F32)
    u = jnp.dot(h_bf, wu_ref[...], preferred_element_type=F32)
    hid = (g * jax.nn.sigmoid(g) * u).astype(BF16)
    return jnp.dot(hid, wd_ref[...], preferred_element_type=F32)


def _ffn_dense_kernel(x_ref, g_ref, wg_ref, wu_ref, wd_ref, o_ref):
    x = x_ref[...]
    h = _rms(x, g_ref[...]).astype(BF16)
    o_ref[...] = x + _swiglu_core(h, wg_ref, wu_ref, wd_ref)


def _ffn_dense(x, g, wg_bf, wu_bf, wd_bf, *, tm):
    nt, d = x.shape
    f = wg_bf.shape[1]
    row = pl.BlockSpec((tm, d), lambda i: (i, 0))
    return pl.pallas_call(
        _ffn_dense_kernel,
        grid=(nt // tm,),
        in_specs=[row, _const_spec((1, d)), _const_spec((d, f)), _const_spec((d, f)), _const_spec((f, d))],
        out_specs=row,
        out_shape=jax.ShapeDtypeStruct((nt, d), F32),
        compiler_params=_cparams(("parallel",)),
        name="ffn_dense",
    )(x, g.reshape(1, d), wg_bf, wu_bf, wd_bf)


def _router_kernel(x_ref, g_ref, wr_ref, h_ref, idx_ref, gate_ref):
    h = _rms(x_ref[...], g_ref[...])
    h_hi, h_lo = _split_bf16(h)
    h_ref[...] = h_hi
    w = wr_ref[...]
    w_hi, w_lo = _split_bf16(w)
    logits = (jnp.dot(h_hi, w_hi, preferred_element_type=F32) + jnp.dot(h_hi, w_lo, preferred_element_type=F32)
              + jnp.dot(h_lo, w_hi, preferred_element_type=F32))
    lane = lax.broadcasted_iota(jnp.int32, logits.shape, 1)
    neg = jnp.float32(-jnp.inf)
    l1 = jnp.where(lane < N_EXPERTS, logits, neg)
    m1 = jnp.max(l1, axis=-1, keepdims=True)
    i1 = jnp.min(jnp.where(l1 == m1, lane, LANES), axis=-1, keepdims=True)
    l2 = jnp.where(lane == i1, neg, l1)
    m2 = jnp.max(l2, axis=-1, keepdims=True)
    i2 = jnp.min(jnp.where(l2 == m2, lane, LANES), axis=-1, keepdims=True)
    e = jnp.exp(m2 - m1)
    w1 = 1.0 / (1.0 + e)
    w2 = e / (1.0 + e)
    idx_ref[...] = jnp.where(lane == 0, i1, jnp.where(lane == 1, i2, 0))
    gate_ref[...] = jnp.where(lane == 0, w1, jnp.where(lane == 1, w2, 0.0))


def _router(x, g, w_router_pad, *, tm):
    nt, d = x.shape
    row = lambda width: pl.BlockSpec((tm, width), lambda i: (i, 0))
    return pl.pallas_call(
        _router_kernel,
        grid=(nt // tm,),
        in_specs=[row(d), _const_spec((1, d)), _const_spec((d, LANES))],
        out_specs=[row(d), row(LANES), row(LANES)],
        out_shape=[jax.ShapeDtypeStruct((nt, d), BF16), jax.ShapeDtypeStruct((nt, LANES), jnp.int32),
                   jax.ShapeDtypeStruct((nt, LANES), F32)],
        compiler_params=_cparams(("parallel",)),
        name="router",
    )(x, g.reshape(1, d), w_router_pad)


def _moe_experts_kernel(te_ref, h_ref, wg_ref, wu_ref, wd_ref, o_ref):
    t = pl.program_id(0)

    @pl.when(te_ref[t] < N_EXPERTS)
    def _():
        o_ref[...] = _swiglu_core(h_ref[...], wg_ref, wu_ref, wd_ref)

    @pl.when(te_ref[t] >= N_EXPERTS)
    def _():
        o_ref[...] = jnp.zeros_like(o_ref)


def _moe_experts(tile_expert, h_sorted, wg_bf, wu_bf, wd_bf, *, tmg):
    p, d = h_sorted.shape
    f = wg_bf.shape[2]
    ex = lambda t, te: (jnp.minimum(te[t], N_EXPERTS - 1), 0, 0)
    return pl.pallas_call(
        _moe_experts_kernel,
        grid_spec=pltpu.PrefetchScalarGridSpec(
            num_scalar_prefetch=1,
            grid=(p // tmg,),
            in_specs=[pl.BlockSpec((tmg, d), lambda t, te: (t, 0)),
                      pl.BlockSpec((None, d, f), ex), pl.BlockSpec((None, d, f), ex), pl.BlockSpec((None, f, d), ex)],
            out_specs=pl.BlockSpec((tmg, d), lambda t, te: (t, 0)),
        ),
        out_shape=jax.ShapeDtypeStruct((p, d), F32),
        compiler_params=_cparams(("arbitrary",)),
        name="moe_experts",
    )(tile_expert, h_sorted, wg_bf, wu_bf, wd_bf)


def _moe_combine_kernel(x_ref, y1_ref, y2_ref, gate_ref, o_ref):
    gate = gate_ref[...]
    o_ref[...] = x_ref[...] + gate[:, 0:1] * y1_ref[...] + gate[:, 1:2] * y2_ref[...]


def _moe_combine(x, y1, y2, gate, *, tm):
    nt, d = x.shape
    row = lambda width: pl.BlockSpec((tm, width), lambda i: (i, 0))
    return pl.pallas_call(
        _moe_combine_kernel,
        grid=(nt // tm,),
        in_specs=[row(d), row(d), row(d), row(LANES)],
        out_specs=row(d),
        out_shape=jax.ShapeDtypeStruct((nt, d), F32),
        compiler_params=_cparams(("parallel",)),
        name="moe_combine",
    )(x, y1, y2, gate)


def _moe_plan(idx, token_ok, *, tmg):
    nt = idx.shape[0]
    n_slots = nt * TOP_K
    p = (n_slots // tmg + N_EXPERTS) * tmg
    e_flat = jnp.where(token_ok[:, None], idx, N_EXPERTS).reshape(n_slots)
    order = jnp.argsort(e_flat, stable=True).astype(jnp.int32)
    e_sorted = e_flat[order]
    counts = jnp.sum(e_flat[None, :] == jnp.arange(N_EXPERTS + 1, dtype=jnp.int32)[:, None], axis=1).astype(jnp.int32)
    starts = jnp.cumsum(counts) - counts
    padded = ((counts + tmg - 1) // tmg) * tmg
    pstarts = jnp.cumsum(padded) - padded
    dest_sorted = jnp.arange(n_slots, dtype=jnp.int32) + (pstarts - starts)[e_sorted]
    dest_sorted = jnp.where(e_sorted < N_EXPERTS, dest_sorted, p)
    src_token = jnp.zeros((p,), jnp.int32).at[dest_sorted].set(order // TOP_K, mode="drop")
    dest = jnp.zeros((n_slots,), jnp.int32).at[order].set(jnp.minimum(dest_sorted, p - 1)).reshape(nt, TOP_K)
    tile_start = jnp.arange(p // tmg, dtype=jnp.int32) * tmg
    pends = pstarts[:N_EXPERTS] + padded[:N_EXPERTS]
    tile_expert = jnp.sum(tile_start[:, None] >= pends[None, :], axis=1).astype(jnp.int32)
    return src_token, tile_expert, dest


def _final_norm_kernel(x_ref, g_ref, o_ref):
    o_ref[...] = _rms(x_ref[...], g_ref[...])


def _final_norm(x, g, *, tm):
    nt, d = x.shape
    row = pl.BlockSpec((tm, d), lambda i: (i, 0))
    return pl.pallas_call(
        _final_norm_kernel,
        grid=(nt // tm,),
        in_specs=[row, _const_spec((1, d))],
        out_specs=row,
        out_shape=jax.ShapeDtypeStruct((nt, d), F32),
        compiler_params=_cparams(("parallel",)),
        name="final_norm",
    )(x, g.reshape(1, d))


def kernel(x_prompt, x_sample, cache_k, cache_v, state_conv, page_table, meta_tokens, g_mix, w_in, sb_bias, conv_w, conv_b, conv_ln_g, conv_ln_b, g_out_attn, g_out_conv, w_out, g_ffn, w_gate_dense, w_up_dense, w_down_dense, w_router, w_gate_moe, w_up_moe, w_down_moe, g_final):
    b, seq, d = x_prompt.shape
    bd, t_new, _ = x_sample.shape
    depth = g_mix.shape[0]
    n_phys, page = cache_k.shape[1], cache_k.shape[2]
    cw = state_conv.shape[-1]
    n_real = N_META + seq
    seq_pad = -(-n_real // ATT_BLOCK) * ATT_BLOCK
    n_padrows = seq_pad - n_real
    assert n_padrows >= CONV_STATE, "the zero rows in front of each sequence must cover the conv prefix"
    n_prompt_rows = b * seq_pad
    n_sample_rows = bd * t_new
    nt = n_prompt_rows + n_sample_rows
    tm = 256
    assert nt % tm == 0 and seq_pad % ATT_BLOCK == 0

    xp = jnp.concatenate([jnp.zeros((b, n_padrows, d), F32), jnp.broadcast_to(meta_tokens[None], (b, N_META, d)), x_prompt], axis=1)
    x = jnp.concatenate([xp.reshape(n_prompt_rows, d), x_sample.reshape(n_sample_rows, d)], axis=0)
    row_id = jnp.arange(nt, dtype=jnp.int32)
    token_ok = (row_id >= n_prompt_rows) | (row_id % seq_pad >= n_padrows)

    cache_k2 = cache_k.reshape(depth, n_phys, page, SB_WIDTH)
    cache_v2 = cache_v.reshape(depth, n_phys, page, SB_WIDTH)
    tri = _suffix_tri(ATT_BLOCK)
    tri_t = tri[:, :ATT_BLOCK].T
    tmg = 256

    kp_l, vp_l, ks_l, vs_l, cp_l, cs_l = [], [], [], [], [], []
    for l in range(depth):
        q_bf, k_bf, v_bf, k_f, v_f, glu = _in_proj(x, g_mix[l], w_in[l].astype(BF16), tm=tm, n_batch=b,
                                                   seq_pad=seq_pad, n_padrows=n_padrows)
        attn_p = _sb_prompt(q_bf, k_bf, v_bf, sb_bias[l], tri, n_batch=b, seq_pad=seq_pad)
        smp = lambda a: a[n_prompt_rows:].reshape(bd, t_new, a.shape[-1])
        bias_cols = jnp.tile(sb_bias[l], t_new).reshape(1, t_new * SB_HEADS)
        attn_s = _sb_sample(page_table, smp(q_bf), smp(k_f), smp(v_f), cache_k2, cache_v2, bias_cols, tri_t, l)
        conv_p = _conv_prompt(glu, conv_w[l], conv_b[l], conv_ln_g[l], conv_ln_b[l], n_batch=b, seq_pad=seq_pad, tc=ATT_BLOCK)
        glu_s = smp(glu)
        conv_s = _conv_sample(state_conv[l].transpose(1, 0, 2), glu_s.transpose(1, 0, 2), conv_w[l], conv_b[l],
                              conv_ln_g[l], conv_ln_b[l]).transpose(1, 0, 2)
        attn = jnp.concatenate([attn_p, attn_s.reshape(n_sample_rows, SB_WIDTH)], axis=0)
        conv = jnp.concatenate([conv_p, conv_s.reshape(n_sample_rows, cw)], axis=0)
        x = _merge(x, attn, conv, g_out_attn[l], g_out_conv[l], w_out[l].astype(BF16), tm=tm)

        j = l // 2
        if l % 2 == 0:
            x = _ffn_dense(x, g_ffn[l], w_gate_dense[j].astype(BF16), w_up_dense[j].astype(BF16),
                           w_down_dense[j].astype(BF16), tm=tm)
        else:
            wr = jnp.zeros((d, LANES), F32).at[:, :N_EXPERTS].set(w_router[j])
            h_bf, idx, gate = _router(x, g_ffn[l], wr, tm=tm)
            src_token, tile_expert, dest = _moe_plan(idx[:, :TOP_K], token_ok, tmg=tmg)
            y = _moe_experts(tile_expert, h_bf[src_token], w_gate_moe[j].astype(BF16), w_up_moe[j].astype(BF16),
                             w_down_moe[j].astype(BF16), tmg=tmg)
            gate = jnp.where(token_ok[:, None], gate, 0.0)
            x = _moe_combine(x, y[dest[:, 0]], y[dest[:, 1]], gate, tm=tm)

        heads = lambda a, rows: a.reshape(rows + (SB_HEADS, SB_HEAD_DIM))
        kp_l.append(heads(k_f[:n_prompt_rows].reshape(b, seq_pad, SB_WIDTH)[:, n_padrows:], (b, n_real)))
        vp_l.append(heads(v_f[:n_prompt_rows].reshape(b, seq_pad, SB_WIDTH)[:, n_padrows:], (b, n_real)))
        ks_l.append(heads(smp(k_f), (bd, t_new)))
        vs_l.append(heads(smp(v_f), (bd, t_new)))
        cp_l.append(glu[:n_prompt_rows].reshape(b, seq_pad, cw)[:, seq_pad - CONV_STATE:])
        cs_l.append(jnp.concatenate([state_conv[l], glu_s], axis=1)[:, -CONV_STATE:])

    y = _final_norm(x, g_final, tm=tm)
    y_prompt = y[:n_prompt_rows].reshape(b, seq_pad, d)[:, seq_pad - seq:]
    y_sample = y[n_prompt_rows:].reshape(bd, t_new, d)
    return (y_prompt, y_sample, jnp.stack(kp_l), jnp.stack(vp_l), jnp.stack(ks_l), jnp.stack(vs_l),
            jnp.stack(cp_l), jnp.stack(cs_l))
```

```python
import functools

import jax
import jax.numpy as jnp
from jax import lax
from jax.experimental import pallas as pl
from jax.experimental.pallas import tpu as pltpu

F32 = jnp.float32
BF16 = jnp.bfloat16

N_META = 16
SB_HEADS = 8
SB_HEAD_DIM = 64
SB_WIDTH = SB_HEADS * SB_HEAD_DIM
CONV_KERNEL = 31
CONV_STATE = CONV_KERNEL - 1
N_EXPERTS = 8
TOP_K = 2
RMS_EPS = 1e-6
LN_EPS = 1e-5
LOG2E = 1.4426950408889634

LANES = 128
ATT_BLOCK = 128
SUM_BLOCK = 256
CONV_HALO = 32
VMEM_LIMIT = 56 * 1024 * 1024


def _cparams(sem):
    return pltpu.CompilerParams(dimension_semantics=sem, vmem_limit_bytes=VMEM_LIMIT)


def _rms(x, g):
    return x * lax.rsqrt(jnp.mean(x * x, axis=-1, keepdims=True) + RMS_EPS) * g


def _const_spec(shape):
    nd = len(shape)
    return pl.BlockSpec(shape, lambda *_: (0,) * nd)


def _in_proj_kernel(x_ref, g_ref, w_ref, qb_ref, kb_ref, vb_ref, kf_ref, vf_ref, glu_ref,
                    *, tm, n_batch, seq_pad, n_padrows):
    i = pl.program_id(0)
    h = _rms(x_ref[...], g_ref[...])
    r = i * tm + lax.broadcasted_iota(jnp.int32, (tm, 1), 0)
    rb = r
    for b in range(1, n_batch):
        rb = jnp.where(r >= b * seq_pad, r - b * seq_pad, rb)
    valid = (r >= n_batch * seq_pad) | (rb >= n_padrows)
    h = jnp.where(valid, h, 0.0).astype(BF16)
    w = SB_WIDTH
    q = jnp.dot(h, w_ref[:, 0:w], preferred_element_type=F32)
    qb_ref[...] = (q * (LOG2E * SB_HEAD_DIM ** -0.5)).astype(BF16)
    k = jnp.dot(h, w_ref[:, w:2 * w], preferred_element_type=F32)
    kf_ref[...] = k
    kb_ref[...] = k.astype(BF16)
    v = jnp.dot(h, w_ref[:, 2 * w:3 * w], preferred_element_type=F32)
    vf_ref[...] = v
    vb_ref[...] = v.astype(BF16)
    cw = (w_ref.shape[1] - 3 * w) // 2
    a = jnp.dot(h, w_ref[:, 3 * w:3 * w + cw], preferred_element_type=F32)
    g = jnp.dot(h, w_ref[:, 3 * w + cw:3 * w + 2 * cw], preferred_element_type=F32)
    glu_ref[...] = a * jax.nn.sigmoid(g)


def _in_proj(x, g, w_bf, *, tm, n_batch, seq_pad, n_padrows):
    nt, d = x.shape
    e = w_bf.shape[1]
    cw = (e - 3 * SB_WIDTH) // 2
    row = lambda width: pl.BlockSpec((tm, width), lambda i: (i, 0))
    return pl.pallas_call(
        functools.partial(_in_proj_kernel, tm=tm, n_batch=n_batch, seq_pad=seq_pad, n_padrows=n_padrows),
        grid=(nt // tm,),
        in_specs=[row(d), _const_spec((1, d)), _const_spec((d, e))],
        out_specs=[row(SB_WIDTH)] * 5 + [row(cw)],
        out_shape=[jax.ShapeDtypeStruct((nt, SB_WIDTH), BF16)] * 3
        + [jax.ShapeDtypeStruct((nt, SB_WIDTH), F32)] * 2 + [jax.ShapeDtypeStruct((nt, cw), F32)],
        compiler_params=_cparams(("parallel",)),
        name="in_proj",
    )(x, g.reshape(1, d), w_bf)


def _softplus2(z):
    return jnp.maximum(z, 0.0) + jnp.log2(1.0 + jnp.exp2(-jnp.abs(z)))


def _split_bf16(x):
    hi = x.astype(BF16)
    lo = (x - hi.astype(F32)).astype(BF16)
    return hi, lo


def _suffix_tri(n):
    j = lax.broadcasted_iota(jnp.int32, (n, n), 0)
    s = lax.broadcasted_iota(jnp.int32, (n, n), 1)
    return jnp.where(j >= s, 1.0, 0.0).astype(BF16)


def _sb_weights(z_blocks, valid_blocks, tri, run):
    m = z_blocks[0].shape[0]
    parts = []
    for z, valid in zip(z_blocks, valid_blocks):
        sp = _softplus2(z)
        if valid is not None:
            sp = jnp.where(valid, sp, 0.0)
        parts.append(sp.astype(BF16))
    cs = jnp.dot(jnp.concatenate(parts, axis=0), tri, preferred_element_type=F32)
    a_parts = [None] * len(z_blocks)
    for c in reversed(range(len(z_blocks))):
        blk = cs[c * m:(c + 1) * m]
        a = jnp.exp2(z_blocks[c] - blk - run)
        if valid_blocks[c] is not None:
            a = jnp.where(valid_blocks[c], a, 0.0)
        a_parts[c] = a.astype(BF16)
        run = run + blk[:, 0:1]
    return jnp.concatenate(a_parts, axis=1), run


def _sb_sweep(z_blocks, valid_blocks, v_rows, tri, run):
    a, run = _sb_weights(z_blocks, valid_blocks, tri, run)
    return jnp.dot(a, v_rows, preferred_element_type=F32), run


def _sb_prompt_kernel(bias_ref, q_ref, k_ref, v_ref, tri_ref, o_ref, qs_ref, acc_ref, *, span):
    hp = pl.program_id(1)
    qb = pl.program_id(2)
    n = ATT_BLOCK
    m = 2 * n
    sb = min(SUM_BLOCK, span)
    lane = lax.broadcasted_iota(jnp.int32, (1, LANES), 1)
    q = q_ref[...].astype(F32)
    qs_ref[0:n, :] = jnp.where(lane < SB_HEAD_DIM, q, 0.0).astype(BF16)
    qs_ref[n:m, :] = jnp.where(lane >= SB_HEAD_DIM, q, 0.0).astype(BF16)
    srow = lax.broadcasted_iota(jnp.int32, (m, 1), 0)
    bias = jnp.where(srow < n, bias_ref[hp * 2], bias_ref[hp * 2 + 1])
    q_pos = qb * n + jnp.where(srow < n, srow, srow - n)
    key_lane = lax.broadcasted_iota(jnp.int32, (1, sb), 1)
    hi_end = (qb + 1) * n
    n_sweeps = (hi_end + span - 1) // span

    def start_of(t):
        return pl.multiple_of(jnp.maximum(hi_end - (t + 1) * span, 0), n)

    def logits(t):
        return lax.dot_general(qs_ref[...], k_ref[pl.ds(start_of(t), span), :], (((1,), (1,)), ((), ())),
                               preferred_element_type=F32)

    def weights(z, t, limit, run):
        start = start_of(t)
        zb = [z[:, c * sb:(c + 1) * sb] + bias for c in range(span // sb)]
        vb = [None if limit is None else (start + c * sb + key_lane) < limit for c in range(span // sb)]
        return _sb_weights(zb, vb, tri_ref[...], run)

    def values(a, t):
        return jnp.dot(a, v_ref[pl.ds(start_of(t), span), :], preferred_element_type=F32)

    z0 = logits(0)
    z1 = logits(1)
    a0, run0 = weights(z0, 0, q_pos, jnp.zeros((m, 1), F32))
    acc_ref[...] = jnp.zeros_like(acc_ref)

    def step(t, carry):
        z_cur, a_prev, run = carry
        z_next = logits(t + 1)
        acc_ref[...] += values(a_prev, t - 1)
        a_cur, run = weights(z_cur, t, None, run)
        return z_next, a_cur, run

    n_mid = jnp.maximum(n_sweeps - 2, 0)
    carry = lax.fori_loop(0, n_mid // 2, lambda u, c: step(2 * u + 2, step(2 * u + 1, c)), (z1, a0, run0))
    z_last, a_prev, run = lax.fori_loop(0, n_mid % 2, lambda u, c: step(n_mid, c), carry)
    acc_ref[...] += values(a_prev, jnp.maximum(n_sweeps - 2, 0))

    @pl.when(n_sweeps >= 2)
    def _():
        last = n_sweeps - 1
        done_from = hi_end - last * span
        a_last, _ = weights(z_last, last, jnp.full((m, 1), done_from, jnp.int32), run)
        acc_ref[...] += values(a_last, last)

    o_ref[...] = jnp.where(lane < SB_HEAD_DIM, acc_ref[0:n, :], acc_ref[n:m, :])


def _sb_prompt(q_bf, k_bf, v_bf, bias2, tri, *, n_batch, seq_pad):
    n = ATT_BLOCK
    nqb = seq_pad // n
    n_hp = SB_WIDTH // LANES
    span = min(2 * SUM_BLOCK, seq_pad)
    sb = min(SUM_BLOCK, span)
    assert span % sb == 0 and tri.shape == (sb, sb)
    return pl.pallas_call(
        functools.partial(_sb_prompt_kernel, span=span),
        grid_spec=pltpu.PrefetchScalarGridSpec(
            num_scalar_prefetch=1,
            grid=(n_batch, n_hp, nqb),
            in_specs=[
                pl.BlockSpec((n, LANES), lambda b, h, i, *_: (b * nqb + i, h)),
                pl.BlockSpec((seq_pad, LANES), lambda b, h, i, *_: (b, h)),
                pl.BlockSpec((seq_pad, LANES), lambda b, h, i, *_: (b, h)),
                pl.BlockSpec((sb, sb), lambda b, h, i, *_: (0, 0)),
            ],
            out_specs=pl.BlockSpec((n, LANES), lambda b, h, i, *_: (b * nqb + i, h)),
            scratch_shapes=[pltpu.VMEM((2 * n, LANES), BF16), pltpu.VMEM((2 * n, LANES), F32)],
        ),
        out_shape=jax.ShapeDtypeStruct((n_batch * seq_pad, SB_WIDTH), F32),
        compiler_params=_cparams(("parallel", "parallel", "arbitrary")),
        name="sb_prompt",
    )(bias2, q_bf, k_bf, v_bf, tri)


def _sb_sample_kernel(*refs, t_new, pages_per_step):
    pg = pages_per_step
    pt_ref, bias_ref, q_ref, kn_ref, vn_ref = refs[:5]
    kc_refs = refs[5:5 + pg]
    vc_refs = refs[5 + pg:5 + 2 * pg]
    tri_ref, o_ref, carry_ref, acc_ref = refs[5 + 2 * pg:]
    del pt_ref
    g = pl.program_id(1)
    sb = SUM_BLOCK
    m = t_new * SB_HEADS
    row = lax.broadcasted_iota(jnp.int32, (m, 1), 0)
    q = q_ref[0]
    bias = bias_ref[...]

    def sweep(k2d, v2d, run, new_keys):
        lanes = k2d.shape[0]
        z = lax.dot_general(q, k2d, (((1,), (1,)), ((), ())), preferred_element_type=F32)
        zb, vb = [], []
        for c in range(lanes // sb):
            kl = c * sb + lax.broadcasted_iota(jnp.int32, (1, sb), 1)
            valid = (kl % SB_HEADS) == (row % SB_HEADS)
            if new_keys:
                valid = valid & ((kl // SB_HEADS) < (row // SB_HEADS))
            zb.append(z[:, c * sb:(c + 1) * sb] + bias)
            vb.append(valid)
        return _sb_sweep(zb, vb, v2d, tri_ref[...], run)

    @pl.when(g == 0)
    def _():
        pad = jnp.zeros((sb - kn_ref.shape[1], SB_HEAD_DIM), F32)
        k2d = jnp.concatenate([kn_ref[0], pad], axis=0).astype(BF16)
        v2d = jnp.concatenate([vn_ref[0], pad], axis=0).astype(BF16)
        out, run = sweep(k2d, v2d, jnp.zeros((m, 1), F32), True)
        acc_ref[...] = out
        carry_ref[...] = run

    run = carry_ref[...]
    acc = acc_ref[...]
    for kc_ref, vc_ref in zip(kc_refs, vc_refs):
        rows = kc_ref.shape[0] * kc_ref.shape[1]
        k2d = kc_ref[...].reshape(rows, SB_HEAD_DIM).astype(BF16)
        v2d = vc_ref[...].reshape(rows, SB_HEAD_DIM).astype(BF16)
        out, run = sweep(k2d, v2d, run, False)
        acc = acc + out
    carry_ref[...] = run
    acc_ref[...] = acc
    o_ref[0] = acc


def _sb_sample(page_table, q2, k_new2, v_new2, cache_k, cache_v, bias_rows, tri, layer, *, t_new):
    bd, m, dh = q2.shape
    n_pages = page_table.shape[1]
    page = cache_k.shape[2]
    assert page * SB_HEADS % SUM_BLOCK == 0 and m <= SUM_BLOCK and tri.shape == (SUM_BLOCK, SUM_BLOCK)
    pg = 8 if n_pages % 8 == 0 else n_pages
    seq = lambda: pl.BlockSpec((1, m, dh), lambda b, g, pt: (b, 0, 0))

    def cache(i):
        return pl.BlockSpec((None, None, page, SB_HEADS, dh),
                            lambda b, g, pt: (layer, pt[b, n_pages - 1 - (g * pg + i)], 0, 0, 0))

    return pl.pallas_call(
        functools.partial(_sb_sample_kernel, t_new=t_new, pages_per_step=pg),
        grid_spec=pltpu.PrefetchScalarGridSpec(
            num_scalar_prefetch=1,
            grid=(bd, n_pages // pg),
            in_specs=[pl.BlockSpec((m, 1), lambda b, g, pt: (0, 0)), seq(), seq(), seq()]
            + [cache(i) for i in range(pg)] + [cache(i) for i in range(pg)]
            + [pl.BlockSpec((SUM_BLOCK, SUM_BLOCK), lambda b, g, pt: (0, 0))],
            out_specs=seq(),
            scratch_shapes=[pltpu.VMEM((m, 1), F32), pltpu.VMEM((m, dh), F32)],
        ),
        out_shape=jax.ShapeDtypeStruct((bd, m, dh), F32),
        compiler_params=_cparams(("parallel", "arbitrary")),
        name="sb_sample",
    )(page_table, bias_rows, q2, k_new2, v_new2, *([cache_k] * pg), *([cache_v] * pg), tri)


def _ln_silu(y, g, b):
    mu = jnp.mean(y, axis=-1, keepdims=True)
    d = y - mu
    var = jnp.mean(d * d, axis=-1, keepdims=True)
    y = d * lax.rsqrt(var + LN_EPS) * g + b
    return y * jax.nn.sigmoid(y)


def _conv_prompt_kernel(glu_ref, w_ref, cb_ref, lg_ref, lb_ref, o_ref, win_ref, *, tc):
    i = pl.program_id(1)
    halo = CONV_HALO

    @pl.when(i == 0)
    def _():
        win_ref[0:halo, :] = jnp.zeros((halo, win_ref.shape[1]), F32)

    win_ref[halo:halo + tc, :] = glu_ref[...]
    acc = jnp.zeros((tc, win_ref.shape[1]), F32) + cb_ref[...]
    for j in range(CONV_KERNEL):
        off = halo - CONV_STATE + j
        acc = acc + win_ref[off:off + tc, :] * w_ref[j:j + 1, :]
    o_ref[...] = _ln_silu(acc, lg_ref[...], lb_ref[...])
    win_ref[0:halo, :] = win_ref[tc:tc + halo, :]


def _conv_prompt(glu, conv_w, conv_b, ln_g, ln_b, *, n_batch, seq_pad, tc):
    cw = glu.shape[1]
    nb = seq_pad // tc
    vec = lambda: _const_spec((1, cw))
    return pl.pallas_call(
        functools.partial(_conv_prompt_kernel, tc=tc),
        grid=(n_batch, nb),
        in_specs=[pl.BlockSpec((tc, cw), lambda b, i: (b * nb + i, 0)), _const_spec((CONV_KERNEL, cw)), vec(), vec(), vec()],
        out_specs=pl.BlockSpec((tc, cw), lambda b, i: (b * nb + i, 0)),
        out_shape=jax.ShapeDtypeStruct((n_batch * seq_pad, cw), F32),
        scratch_shapes=[pltpu.VMEM((CONV_HALO + tc, cw), F32)],
        compiler_params=_cparams(("parallel", "arbitrary")),
        name="conv_prompt",
    )(glu, conv_w, conv_b.reshape(1, cw), ln_g.reshape(1, cw), ln_b.reshape(1, cw))


def _conv_sample_kernel(st_ref, glu_ref, w_ref, cb_ref, lg_ref, lb_ref, o_ref, *, t_new):
    for t in range(t_new):
        acc = jnp.zeros(o_ref.shape[1:], F32) + cb_ref[...]
        for j in range(CONV_KERNEL):
            r = t + j
            tap = st_ref[r] if r < CONV_STATE else glu_ref[r - CONV_STATE]
            acc = acc + tap * w_ref[j:j + 1, :]
        o_ref[t] = _ln_silu(acc, lg_ref[...], lb_ref[...])


def _conv_sample(state_t, glu_t, conv_w, conv_b, ln_g, ln_b):
    t_new, bd, cw = glu_t.shape
    vec = lambda: _const_spec((1, cw))
    return pl.pallas_call(
        functools.partial(_conv_sample_kernel, t_new=t_new),
        grid=(1,),
        in_specs=[_const_spec(state_t.shape), _const_spec(glu_t.shape), _const_spec((CONV_KERNEL, cw)), vec(), vec(), vec()],
        out_specs=_const_spec((t_new, bd, cw)),
        out_shape=jax.ShapeDtypeStruct((t_new, bd, cw), F32),
        compiler_params=_cparams(("arbitrary",)),
        name="conv_sample",
    )(state_t, glu_t, conv_w, conv_b.reshape(1, cw), ln_g.reshape(1, cw), ln_b.reshape(1, cw))


def _merge_kernel(x_ref, a_ref, c_ref, ga_ref, gc_ref, wa_ref, wc_ref, o_ref):
    na = _rms(a_ref[...], ga_ref[...]).astype(BF16)
    nc = _rms(c_ref[...], gc_ref[...]).astype(BF16)
    y = jnp.dot(na, wa_ref[...], preferred_element_type=F32)
    y = y + jnp.dot(nc, wc_ref[...], preferred_element_type=F32)
    o_ref[...] = x_ref[...] + y


def _merge(x, attn, conv, g_a, g_c, w_out_bf, *, tm):
    nt, d = x.shape
    wa = attn.shape[1]
    wc = conv.shape[1]
    row = lambda width: pl.BlockSpec((tm, width), lambda i: (i, 0))
    return pl.pallas_call(
        _merge_kernel,
        grid=(nt // tm,),
        in_specs=[row(d), row(wa), row(wc), _const_spec((1, wa)), _const_spec((1, wc)),
                  pl.BlockSpec((wa, d), lambda i: (0, 0)), pl.BlockSpec((wc, d), lambda i: (wa // wc, 0))],
        out_specs=row(d),
        out_shape=jax.ShapeDtypeStruct((nt, d), F32),
        compiler_params=_cparams(("parallel",)),
        name="merge",
    )(x, attn, conv, g_a.reshape(1, wa), g_c.reshape(1, wc), w_out_bf, w_out_bf)


def _swiglu_core(h_bf, wg_ref, wu_ref, wd_ref):
    g = jnp.dot(h_bf, wg_ref[...], preferred_element_type=F32)
    u = jnp.dot(h_bf, wu_ref[...], preferred_element_type=F32)
    hid = (g * jax.nn.sigmoid(g) * u).astype(BF16)
    return jnp.dot(hid, wd_ref[...], preferred_element_type=F32)


def _ffn_dense_kernel(x_ref, g_ref, wg_ref, wu_ref, wd_ref, o_ref):
    x = x_ref[...]
    h = _rms(x, g_ref[...]).astype(BF16)
    o_ref[...] = x + _swiglu_core(h, wg_ref, wu_ref, wd_ref)


def _ffn_dense(x, g, wg_bf, wu_bf, wd_bf, *, tm):
    nt, d = x.shape
    f = wg_bf.shape[1]
    row = pl.BlockSpec((tm, d), lambda i: (i, 0))
    return pl.pallas_call(
        _ffn_dense_kernel,
        grid=(nt // tm,),
        in_specs=[row, _const_spec((1, d)), _const_spec((d, f)), _const_spec((d, f)), _const_spec((f, d))],
        out_specs=row,
        out_shape=jax.ShapeDtypeStruct((nt, d), F32),
        compiler_params=_cparams(("parallel",)),
        name="ffn_dense",
    )(x, g.reshape(1, d), wg_bf, wu_bf, wd_bf)


def _router_kernel(x_ref, g_ref, wr_ref, h_ref, idx_ref, gate_ref):
    h = _rms(x_ref[...], g_ref[...])
    h_hi, h_lo = _split_bf16(h)
    h_ref[...] = h_hi
    w = wr_ref[...]
    w_hi, w_lo = _split_bf16(w)
    logits = (jnp.dot(h_hi, w_hi, preferred_element_type=F32) + jnp.dot(h_hi, w_lo, preferred_element_type=F32)
              + jnp.dot(h_lo, w_hi, preferred_element_type=F32))
    lane = lax.broadcasted_iota(jnp.int32, logits.shape, 1)
    neg = jnp.float32(-jnp.inf)
    l1 = jnp.where(lane < N_EXPERTS, logits, neg)
    m1 = jnp.max(l1, axis=-1, keepdims=True)
    i1 = jnp.min(jnp.where(l1 == m1, lane, LANES), axis=-1, keepdims=True)
    l2 = jnp.where(lane == i1, neg, l1)
    m2 = jnp.max(l2, axis=-1, keepdims=True)
    i2 = jnp.min(jnp.where(l2 == m2, lane, LANES), axis=-1, keepdims=True)
    e = jnp.exp(m2 - m1)
    w1 = 1.0 / (1.0 + e)
    w2 = e / (1.0 + e)
    idx_ref[...] = jnp.where(lane == 0, i1, jnp.where(lane == 1, i2, 0))
    gate_ref[...] = jnp.where(lane == 0, w1, jnp.where(lane == 1, w2, 0.0))


def _router(x, g, w_router_pad, *, tm):
    nt, d = x.shape
    row = lambda width: pl.BlockSpec((tm, width), lambda i: (i, 0))
    return pl.pallas_call(
        _router_kernel,
        grid=(nt // tm,),
        in_specs=[row(d), _const_spec((1, d)), _const_spec((d, LANES))],
        out_specs=[row(d), row(LANES), row(LANES)],
        out_shape=[jax.ShapeDtypeStruct((nt, d), BF16), jax.ShapeDtypeStruct((nt, LANES), jnp.int32),
                   jax.ShapeDtypeStruct((nt, LANES), F32)],
        compiler_params=_cparams(("parallel",)),
        name="router",
    )(x, g.reshape(1, d), w_router_pad)


def _moe_experts_kernel(te_ref, h_ref, wg_ref, wu_ref, wd_ref, o_ref):
    t = pl.program_id(0)

    @pl.when(te_ref[t] < N_EXPERTS)
    def _():
        o_ref[...] = _swiglu_core(h_ref[...], wg_ref, wu_ref, wd_ref)

    @pl.when(te_ref[t] >= N_EXPERTS)
    def _():
        o_ref[...] = jnp.zeros_like(o_ref)


def _moe_experts(tile_expert, h_sorted, wg_bf, wu_bf, wd_bf, *, tmg):
    p, d = h_sorted.shape
    f = wg_bf.shape[2]
    ex = lambda t, te: (jnp.minimum(te[t], N_EXPERTS - 1), 0, 0)
    return pl.pallas_call(
        _moe_experts_kernel,
        grid_spec=pltpu.PrefetchScalarGridSpec(
            num_scalar_prefetch=1,
            grid=(p // tmg,),
            in_specs=[pl.BlockSpec((tmg, d), lambda t, te: (t, 0)),
                      pl.BlockSpec((None, d, f), ex), pl.BlockSpec((None, d, f), ex), pl.BlockSpec((None, f, d), ex)],
            out_specs=pl.BlockSpec((tmg, d), lambda t, te: (t, 0)),
        ),
        out_shape=jax.ShapeDtypeStruct((p, d), F32),
        compiler_params=_cparams(("arbitrary",)),
        name="moe_experts",
    )(tile_expert, h_sorted, wg_bf, wu_bf, wd_bf)


def _moe_combine_kernel(x_ref, y1_ref, y2_ref, gate_ref, o_ref):
    gate = gate_ref[...]
    o_ref[...] = x_ref[...] + gate[:, 0:1] * y1_ref[...] + gate[:, 1:2] * y2_ref[...]


def _moe_combine(x, y1, y2, gate, *, tm):
    nt, d = x.shape
    row = lambda width: pl.BlockSpec((tm, width), lambda i: (i, 0))
    return pl.pallas_call(
        _moe_combine_kernel,
        grid=(nt // tm,),
        in_specs=[row(d), row(d), row(d), row(LANES)],
        out_specs=row(d),
        out_shape=jax.ShapeDtypeStruct((nt, d), F32),
        compiler_params=_cparams(("parallel",)),
        name="moe_combine",
    )(x, y1, y2, gate)


def _moe_plan(idx, token_ok, *, tmg):
    nt = idx.shape[0]
    n_slots = nt * TOP_K
    p = (n_slots // tmg + N_EXPERTS) * tmg
    e_flat = jnp.where(token_ok[:, None], idx, N_EXPERTS).reshape(n_slots)
    order = jnp.argsort(e_flat, stable=True).astype(jnp.int32)
    e_sorted = e_flat[order]
    counts = jnp.sum(e_flat[None, :] == jnp.arange(N_EXPERTS + 1, dtype=jnp.int32)[:, None], axis=1).astype(jnp.int32)
    starts = jnp.cumsum(counts) - counts
    padded = ((counts + tmg - 1) // tmg) * tmg
    pstarts = jnp.cumsum(padded) - padded
    dest_sorted = jnp.arange(n_slots, dtype=jnp.int32) + (pstarts - starts)[e_sorted]
    dest_sorted = jnp.where(e_sorted < N_EXPERTS, dest_sorted, p)
    src_token = jnp.zeros((p,), jnp.int32).at[dest_sorted].set(order // TOP_K, mode="drop")
    dest = jnp.zeros((n_slots,), jnp.int32).at[order].set(jnp.minimum(dest_sorted, p - 1)).reshape(nt, TOP_K)
    tile_start = jnp.arange(p // tmg, dtype=jnp.int32) * tmg
    pends = pstarts[:N_EXPERTS] + padded[:N_EXPERTS]
    tile_expert = jnp.sum(tile_start[:, None] >= pends[None, :], axis=1).astype(jnp.int32)
    return src_token, tile_expert, dest


def _final_norm_kernel(x_ref, g_ref, o_ref):
    o_ref[...] = _rms(x_ref[...], g_ref[...])


def _final_norm(x, g, *, tm):
    nt, d = x.shape
    row = pl.BlockSpec((tm, d), lambda i: (i, 0))
    return pl.pallas_call(
        _final_norm_kernel,
        grid=(nt // tm,),
        in_specs=[row, _const_spec((1, d))],
        out_specs=row,
        out_shape=jax.ShapeDtypeStruct((nt, d), F32),
        compiler_params=_cparams(("parallel",)),
        name="final_norm",
    )(x, g.reshape(1, d))


def kernel(x_prompt, x_sample, cache_k, cache_v, state_conv, page_table, meta_tokens, g_mix, w_in, sb_bias, conv_w, conv_b, conv_ln_g, conv_ln_b, g_out_attn, g_out_conv, w_out, g_ffn, w_gate_dense, w_up_dense, w_down_dense, w_router, w_gate_moe, w_up_moe, w_down_moe, g_final):
    b, seq, d = x_prompt.shape
    bd, t_new, _ = x_sample.shape
    depth = g_mix.shape[0]
    cw = state_conv.shape[-1]
    n_real = N_META + seq
    seq_pad = -(-n_real // ATT_BLOCK) * ATT_BLOCK
    n_padrows = seq_pad - n_real
    assert n_padrows >= CONV_STATE, "the zero rows in front of each sequence must cover the conv prefix"
    n_prompt_rows = b * seq_pad
    n_sample_rows = bd * t_new
    nt = n_prompt_rows + n_sample_rows
    tm = 256
    assert nt % tm == 0 and seq_pad % ATT_BLOCK == 0

    xp = jnp.concatenate([jnp.zeros((b, n_padrows, d), F32), jnp.broadcast_to(meta_tokens[None], (b, N_META, d)), x_prompt], axis=1)
    x = jnp.concatenate([xp.reshape(n_prompt_rows, d), x_sample.reshape(n_sample_rows, d)], axis=0)
    row_id = jnp.arange(nt, dtype=jnp.int32)
    token_ok = (row_id >= n_prompt_rows) | (row_id % seq_pad >= n_padrows)

    tri = _suffix_tri(min(SUM_BLOCK, seq_pad))
    tmg = 256

    kp_l, vp_l, ks_l, vs_l, cp_l, cs_l = [], [], [], [], [], []
    for l in range(depth):
        q_bf, k_bf, v_bf, k_f, v_f, glu = _in_proj(x, g_mix[l], w_in[l].astype(BF16), tm=tm, n_batch=b,
                                                   seq_pad=seq_pad, n_padrows=n_padrows)
        bias2 = sb_bias[l] * LOG2E
        attn_p = _sb_prompt(q_bf, k_bf, v_bf, bias2, tri, n_batch=b, seq_pad=seq_pad)
        smp = lambda a: a[n_prompt_rows:].reshape(bd, t_new, a.shape[-1])
        per_head = lambda a: smp(a).reshape(bd, t_new * SB_HEADS, SB_HEAD_DIM)
        bias_rows = jnp.tile(bias2, t_new)[:, None]
        attn_s = _sb_sample(page_table, per_head(q_bf), per_head(k_f), per_head(v_f), cache_k, cache_v, bias_rows, tri, l,
                            t_new=t_new)
        conv_p = _conv_prompt(glu, conv_w[l], conv_b[l], conv_ln_g[l], conv_ln_b[l], n_batch=b, seq_pad=seq_pad, tc=ATT_BLOCK)
        glu_s = smp(glu)
        conv_s = _conv_sample(state_conv[l].transpose(1, 0, 2), glu_s.transpose(1, 0, 2), conv_w[l], conv_b[l],
                              conv_ln_g[l], conv_ln_b[l]).transpose(1, 0, 2)
        attn = jnp.concatenate([attn_p, attn_s.reshape(n_sample_rows, SB_WIDTH)], axis=0)
        conv = jnp.concatenate([conv_p, conv_s.reshape(n_sample_rows, cw)], axis=0)
        x = _merge(x, attn, conv, g_out_attn[l], g_out_conv[l], w_out[l].astype(BF16), tm=tm)

        j = l // 2
        if l % 2 == 0:
            x = _ffn_dense(x, g_ffn[l], w_gate_dense[j].astype(BF16), w_up_dense[j].astype(BF16),
                           w_down_dense[j].astype(BF16), tm=tm)
        else:
            wr = jnp.zeros((d, LANES), F32).at[:, :N_EXPERTS].set(w_router[j])
            h_bf, idx, gate = _router(x, g_ffn[l], wr, tm=tm)
            src_token, tile_expert, dest = _moe_plan(idx[:, :TOP_K], token_ok, tmg=tmg)
            y = _moe_experts(tile_expert, h_bf[src_token], w_gate_moe[j].astype(BF16), w_up_moe[j].astype(BF16),
                             w_down_moe[j].astype(BF16), tmg=tmg)
            gate = jnp.where(token_ok[:, None], gate, 0.0)
            x = _moe_combine(x, y[dest[:, 0]], y[dest[:, 1]], gate, tm=tm)

        heads = lambda a, rows: a.reshape(rows + (SB_HEADS, SB_HEAD_DIM))
        kp_l.append(heads(k_f[:n_prompt_rows].reshape(b, seq_pad, SB_WIDTH)[:, n_padrows:], (b, n_real)))
        vp_l.append(heads(v_f[:n_prompt_rows].reshape(b, seq_pad, SB_WIDTH)[:, n_padrows:], (b, n_real)))
        ks_l.append(heads(smp(k_f), (bd, t_new)))
        vs_l.append(heads(smp(v_f), (bd, t_new)))
        cp_l.append(glu[:n_prompt_rows].reshape(b, seq_pad, cw)[:, seq_pad - CONV_STATE:])
        cs_l.append(jnp.concatenate([state_conv[l], glu_s], axis=1)[:, -CONV_STATE:])

    y = _final_norm(x, g_final, tm=tm)
    y_prompt = y[:n_prompt_rows].reshape(b, seq_pad, d)[:, seq_pad - seq:]
    y_sample = y[n_prompt_rows:].reshape(bd, t_new, d)
    return (y_prompt, y_sample, jnp.stack(kp_l), jnp.stack(vp_l), jnp.stack(ks_l), jnp.stack(vs_l),
            jnp.stack(cp_l), jnp.stack(cs_l))
```

```python
import functools

import jax
import jax.numpy as jnp
from jax import lax
from jax.experimental import pallas as pl
from jax.experimental.pallas import tpu as pltpu

F32 = jnp.float32
BF16 = jnp.bfloat16

N_META = 16
SB_HEADS = 8
SB_HEAD_DIM = 64
SB_WIDTH = SB_HEADS * SB_HEAD_DIM
CONV_KERNEL = 31
CONV_STATE = CONV_KERNEL - 1
N_EXPERTS = 8
TOP_K = 2
RMS_EPS = 1e-6
LN_EPS = 1e-5
LOG2E = 1.4426950408889634

LANES = 128
ATT_BLOCK = 128
SUM_BLOCK = 256
CONV_HALO = 32
VMEM_LIMIT = 56 * 1024 * 1024


def _cparams(sem):
    return pltpu.CompilerParams(dimension_semantics=sem, vmem_limit_bytes=VMEM_LIMIT)


def _rms(x, g):
    return x * lax.rsqrt(jnp.mean(x * x, axis=-1, keepdims=True) + RMS_EPS) * g


def _const_spec(shape):
    nd = len(shape)
    return pl.BlockSpec(shape, lambda *_: (0,) * nd)


def _in_proj_kernel(x_ref, g_ref, w_ref, qb_ref, kb_ref, vb_ref, kf_ref, vf_ref, glu_ref,
                    *, tm, n_batch, seq_pad, n_padrows):
    i = pl.program_id(0)
    h = _rms(x_ref[...], g_ref[...])
    r = i * tm + lax.broadcasted_iota(jnp.int32, (tm, 1), 0)
    rb = r
    for b in range(1, n_batch):
        rb = jnp.where(r >= b * seq_pad, r - b * seq_pad, rb)
    valid = (r >= n_batch * seq_pad) | (rb >= n_padrows)
    h = jnp.where(valid, h, 0.0).astype(BF16)
    w = SB_WIDTH
    q = jnp.dot(h, w_ref[:, 0:w], preferred_element_type=F32)
    qb_ref[...] = (q * (LOG2E * SB_HEAD_DIM ** -0.5)).astype(BF16)
    k = jnp.dot(h, w_ref[:, w:2 * w], preferred_element_type=F32)
    kf_ref[...] = k
    kb_ref[...] = k.astype(BF16)
    v = jnp.dot(h, w_ref[:, 2 * w:3 * w], preferred_element_type=F32)
    vf_ref[...] = v
    vb_ref[...] = v.astype(BF16)
    cw = (w_ref.shape[1] - 3 * w) // 2
    a = jnp.dot(h, w_ref[:, 3 * w:3 * w + cw], preferred_element_type=F32)
    g = jnp.dot(h, w_ref[:, 3 * w + cw:3 * w + 2 * cw], preferred_element_type=F32)
    glu_ref[...] = a * jax.nn.sigmoid(g)


def _in_proj(x, g, w_bf, *, tm, n_batch, seq_pad, n_padrows):
    nt, d = x.shape
    e = w_bf.shape[1]
    cw = (e - 3 * SB_WIDTH) // 2
    row = lambda width: pl.BlockSpec((tm, width), lambda i: (i, 0))
    return pl.pallas_call(
        functools.partial(_in_proj_kernel, tm=tm, n_batch=n_batch, seq_pad=seq_pad, n_padrows=n_padrows),
        grid=(nt // tm,),
        in_specs=[row(d), _const_spec((1, d)), _const_spec((d, e))],
        out_specs=[row(SB_WIDTH)] * 5 + [row(cw)],
        out_shape=[jax.ShapeDtypeStruct((nt, SB_WIDTH), BF16)] * 3
        + [jax.ShapeDtypeStruct((nt, SB_WIDTH), F32)] * 2 + [jax.ShapeDtypeStruct((nt, cw), F32)],
        compiler_params=_cparams(("parallel",)),
        name="in_proj",
    )(x, g.reshape(1, d), w_bf)


def _softplus2(z):
    return jnp.maximum(z, 0.0) + jnp.log2(1.0 + jnp.exp2(-jnp.abs(z)))


def _split_bf16(x):
    hi = x.astype(BF16)
    lo = (x - hi.astype(F32)).astype(BF16)
    return hi, lo


def _suffix_tri(n):
    j = lax.broadcasted_iota(jnp.int32, (n, n), 0)
    s = lax.broadcasted_iota(jnp.int32, (n, n), 1)
    return jnp.where(j >= s, 1.0, 0.0).astype(BF16)


def _sb_weights(z_blocks, valid_blocks, tri, run):
    m = z_blocks[0].shape[0]
    parts = []
    for z, valid in zip(z_blocks, valid_blocks):
        sp = _softplus2(z)
        if valid is not None:
            sp = jnp.where(valid, sp, 0.0)
        parts.append(sp.astype(BF16))
    cs = jnp.dot(jnp.concatenate(parts, axis=0), tri, preferred_element_type=F32)
    a_parts = [None] * len(z_blocks)
    for c in reversed(range(len(z_blocks))):
        blk = cs[c * m:(c + 1) * m]
        a = jnp.exp2(z_blocks[c] - blk - run)
        if valid_blocks[c] is not None:
            a = jnp.where(valid_blocks[c], a, 0.0)
        a_parts[c] = a.astype(BF16)
        run = run + blk[:, 0:1]
    return jnp.concatenate(a_parts, axis=1), run


def _sb_sweep(z_blocks, valid_blocks, v_rows, tri, run):
    a, run = _sb_weights(z_blocks, valid_blocks, tri, run)
    return jnp.dot(a, v_rows, preferred_element_type=F32), run


def _sb_prompt_kernel(bias_ref, q_ref, k_ref, v_ref, tri_ref, o_ref, qs_ref, acc_ref, *, span):
    hp = pl.program_id(1)
    qb = pl.program_id(2)
    n = ATT_BLOCK
    m = 2 * n
    sb = min(SUM_BLOCK, span)
    lane = lax.broadcasted_iota(jnp.int32, (1, LANES), 1)
    q = q_ref[...].astype(F32)
    qs_ref[0:n, :] = jnp.where(lane < SB_HEAD_DIM, q, 0.0).astype(BF16)
    qs_ref[n:m, :] = jnp.where(lane >= SB_HEAD_DIM, q, 0.0).astype(BF16)
    srow = lax.broadcasted_iota(jnp.int32, (m, 1), 0)
    bias = jnp.where(srow < n, bias_ref[hp * 2], bias_ref[hp * 2 + 1])
    q_pos = qb * n + jnp.where(srow < n, srow, srow - n)
    key_lane = lax.broadcasted_iota(jnp.int32, (1, sb), 1)
    hi_end = (qb + 1) * n
    n_sweeps = (hi_end + span - 1) // span

    def start_of(t):
        return pl.multiple_of(jnp.maximum(hi_end - (t + 1) * span, 0), n)

    def logits(t):
        return lax.dot_general(qs_ref[...], k_ref[pl.ds(start_of(t), span), :], (((1,), (1,)), ((), ())),
                               preferred_element_type=F32)

    def weights(z, t, limit, run):
        start = start_of(t)
        zb = [z[:, c * sb:(c + 1) * sb] + bias for c in range(span // sb)]
        vb = [None if limit is None else (start + c * sb + key_lane) < limit for c in range(span // sb)]
        return _sb_weights(zb, vb, tri_ref[...], run)

    def values(a, t):
        return jnp.dot(a, v_ref[pl.ds(start_of(t), span), :], preferred_element_type=F32)

    z0 = logits(0)
    z1 = logits(1)
    a0, run0 = weights(z0, 0, q_pos, jnp.zeros((m, 1), F32))
    acc_ref[...] = jnp.zeros_like(acc_ref)

    def step(t, carry):
        z_cur, a_prev, run = carry
        z_next = logits(t + 1)
        acc_ref[...] += values(a_prev, t - 1)
        a_cur, run = weights(z_cur, t, None, run)
        return z_next, a_cur, run

    n_mid = jnp.maximum(n_sweeps - 2, 0)
    carry = lax.fori_loop(0, n_mid // 2, lambda u, c: step(2 * u + 2, step(2 * u + 1, c)), (z1, a0, run0))
    z_last, a_prev, run = lax.fori_loop(0, n_mid % 2, lambda u, c: step(n_mid, c), carry)
    acc_ref[...] += values(a_prev, jnp.maximum(n_sweeps - 2, 0))

    @pl.when(n_sweeps >= 2)
    def _():
        last = n_sweeps - 1
        done_from = hi_end - last * span
        a_last, _ = weights(z_last, last, jnp.full((m, 1), done_from, jnp.int32), run)
        acc_ref[...] += values(a_last, last)

    o_ref[...] = jnp.where(lane < SB_HEAD_DIM, acc_ref[0:n, :], acc_ref[n:m, :])


def _sb_prompt(q_bf, k_bf, v_bf, bias2, tri, *, n_batch, seq_pad):
    n = ATT_BLOCK
    nqb = seq_pad // n
    n_hp = SB_WIDTH // LANES
    span = min(2 * SUM_BLOCK, seq_pad)
    sb = min(SUM_BLOCK, span)
    assert span % sb == 0 and tri.shape == (sb, sb)
    return pl.pallas_call(
        functools.partial(_sb_prompt_kernel, span=span),
        grid_spec=pltpu.PrefetchScalarGridSpec(
            num_scalar_prefetch=1,
            grid=(n_batch, n_hp, nqb),
            in_specs=[
                pl.BlockSpec((n, LANES), lambda b, h, i, *_: (b * nqb + i, h)),
                pl.BlockSpec((seq_pad, LANES), lambda b, h, i, *_: (b, h)),
                pl.BlockSpec((seq_pad, LANES), lambda b, h, i, *_: (b, h)),
                pl.BlockSpec((sb, sb), lambda b, h, i, *_: (0, 0)),
            ],
            out_specs=pl.BlockSpec((n, LANES), lambda b, h, i, *_: (b * nqb + i, h)),
            scratch_shapes=[pltpu.VMEM((2 * n, LANES), BF16), pltpu.VMEM((2 * n, LANES), F32)],
        ),
        out_shape=jax.ShapeDtypeStruct((n_batch * seq_pad, SB_WIDTH), F32),
        compiler_params=_cparams(("parallel", "parallel", "arbitrary")),
        name="sb_prompt",
    )(bias2, q_bf, k_bf, v_bf, tri)


def _sb_sample_kernel(*refs, t_new, pages_per_step):
    pg = pages_per_step
    pt_ref, bias_ref, q_ref, kn_ref, vn_ref = refs[:5]
    kt_refs = refs[5:5 + pg]
    vt_refs = refs[5 + pg:5 + 2 * pg]
    tri_ref, o_ref, qbd_ref, carry_ref, acc_ref = refs[5 + 2 * pg:]
    del pt_ref
    g = pl.program_id(1)
    sb = SUM_BLOCK
    m = t_new * SB_HEADS
    page = kt_refs[0].shape[2]
    bias = bias_ref[...]
    head_of_lane = lax.broadcasted_iota(jnp.int32, (SB_HEADS, SB_WIDTH), 1) // SB_HEAD_DIM
    head_of_row = lax.broadcasted_iota(jnp.int32, (SB_HEADS, SB_WIDTH), 0)
    nt_dims = (((1,), (1,)), ((), ()))

    @pl.when(g == 0)
    def _():
        q = q_ref[0].astype(F32)
        rows = []
        for t in range(t_new):
            qt = jnp.broadcast_to(q[t:t + 1, :], (SB_HEADS, SB_WIDTH))
            rows.append(jnp.where(head_of_lane == head_of_row, qt, 0.0))
        qbd = jnp.concatenate(rows, axis=0).astype(BF16)
        qbd_ref[...] = qbd
        pad = jnp.zeros((sb - t_new, SB_WIDTH), F32)
        k_new = jnp.concatenate([kn_ref[0], pad], axis=0).astype(BF16)
        v_new = jnp.concatenate([vn_ref[0], pad], axis=0).astype(BF16)
        z = lax.dot_general(qbd, k_new, nt_dims, preferred_element_type=F32) + bias
        s_pos = lax.broadcasted_iota(jnp.int32, (m, sb), 1)
        t_row = lax.broadcasted_iota(jnp.int32, (m, sb), 0) // SB_HEADS
        out, run = _sb_sweep([z], [s_pos < t_row], v_new, tri_ref[...], jnp.zeros((m, 1), F32))
        acc_ref[...] = out
        carry_ref[...] = run

    qbd = qbd_ref[...]
    zs = [jnp.dot(qbd, kt_ref[...].reshape(SB_WIDTH, page).astype(BF16), preferred_element_type=F32)
          for kt_ref in kt_refs]
    z = jnp.concatenate(zs, axis=1) + bias
    n_blocks = pg * page // sb
    a, run = _sb_weights([z[:, c * sb:(c + 1) * sb] for c in range(n_blocks)], [None] * n_blocks, tri_ref[...],
                         carry_ref[...])
    acc = acc_ref[...]
    for i, vt_ref in enumerate(vt_refs):
        vt = vt_ref[...].reshape(SB_WIDTH, page).astype(BF16)
        acc = acc + lax.dot_general(a[:, i * page:(i + 1) * page], vt, nt_dims, preferred_element_type=F32)
    carry_ref[...] = run
    acc_ref[...] = acc

    @pl.when(g == pl.num_programs(1) - 1)
    def _():
        for t in range(t_new):
            blk = acc_ref[t * SB_HEADS:(t + 1) * SB_HEADS, :]
            o_ref[0, t:t + 1, :] = jnp.sum(jnp.where(head_of_lane == head_of_row, blk, 0.0), axis=0, keepdims=True)


def _sb_sample(page_table, q_bf, k_new, v_new, cache_kt, cache_vt, bias_rows, tri, layer):
    bd, t_new, w = q_bf.shape
    m = t_new * SB_HEADS
    n_pages = page_table.shape[1]
    page = cache_kt.shape[4]
    pg = max(p for p in range(1, min(n_pages, 16) + 1) if n_pages % p == 0)
    assert (pg * page) % SUM_BLOCK == 0 and t_new <= SUM_BLOCK and tri.shape == (SUM_BLOCK, SUM_BLOCK)
    seq = lambda: pl.BlockSpec((1, t_new, w), lambda b, g, pt: (b, 0, 0))

    def cache(i):
        return pl.BlockSpec((None, None, SB_HEADS, SB_HEAD_DIM, page),
                            lambda b, g, pt: (layer, pt[b, n_pages - (g + 1) * pg + i], 0, 0, 0))

    return pl.pallas_call(
        functools.partial(_sb_sample_kernel, t_new=t_new, pages_per_step=pg),
        grid_spec=pltpu.PrefetchScalarGridSpec(
            num_scalar_prefetch=1,
            grid=(bd, n_pages // pg),
            in_specs=[pl.BlockSpec((m, 1), lambda b, g, pt: (0, 0)), seq(), seq(), seq()]
            + [cache(i) for i in range(pg)] + [cache(i) for i in range(pg)]
            + [pl.BlockSpec((SUM_BLOCK, SUM_BLOCK), lambda b, g, pt: (0, 0))],
            out_specs=seq(),
            scratch_shapes=[pltpu.VMEM((m, w), BF16), pltpu.VMEM((m, 1), F32), pltpu.VMEM((m, w), F32)],
        ),
        out_shape=jax.ShapeDtypeStruct((bd, t_new, w), F32),
        compiler_params=_cparams(("parallel", "arbitrary")),
        name="sb_sample",
    )(page_table, bias_rows, q_bf, k_new, v_new, *([cache_kt] * pg), *([cache_vt] * pg), tri)


def _ln_silu(y, g, b):
    mu = jnp.mean(y, axis=-1, keepdims=True)
    d = y - mu
    var = jnp.mean(d * d, axis=-1, keepdims=True)
    y = d * lax.rsqrt(var + LN_EPS) * g + b
    return y * jax.nn.sigmoid(y)


def _conv_prompt_kernel(glu_ref, w_ref, cb_ref, lg_ref, lb_ref, o_ref, win_ref, *, tc):
    i = pl.program_id(1)
    halo = CONV_HALO

    @pl.when(i == 0)
    def _():
        win_ref[0:halo, :] = jnp.zeros((halo, win_ref.shape[1]), F32)

    win_ref[halo:halo + tc, :] = glu_ref[...]
    acc = jnp.zeros((tc, win_ref.shape[1]), F32) + cb_ref[...]
    for j in range(CONV_KERNEL):
        off = halo - CONV_STATE + j
        acc = acc + win_ref[off:off + tc, :] * w_ref[j:j + 1, :]
    o_ref[...] = _ln_silu(acc, lg_ref[...], lb_ref[...])
    win_ref[0:halo, :] = win_ref[tc:tc + halo, :]


def _conv_prompt(glu, conv_w, conv_b, ln_g, ln_b, *, n_batch, seq_pad, tc):
    cw = glu.shape[1]
    nb = seq_pad // tc
    vec = lambda: _const_spec((1, cw))
    return pl.pallas_call(
        functools.partial(_conv_prompt_kernel, tc=tc),
        grid=(n_batch, nb),
        in_specs=[pl.BlockSpec((tc, cw), lambda b, i: (b * nb + i, 0)), _const_spec((CONV_KERNEL, cw)), vec(), vec(), vec()],
        out_specs=pl.BlockSpec((tc, cw), lambda b, i: (b * nb + i, 0)),
        out_shape=jax.ShapeDtypeStruct((n_batch * seq_pad, cw), F32),
        scratch_shapes=[pltpu.VMEM((CONV_HALO + tc, cw), F32)],
        compiler_params=_cparams(("parallel", "arbitrary")),
        name="conv_prompt",
    )(glu, conv_w, conv_b.reshape(1, cw), ln_g.reshape(1, cw), ln_b.reshape(1, cw))


def _conv_sample_kernel(st_ref, glu_ref, w_ref, cb_ref, lg_ref, lb_ref, o_ref, *, t_new):
    for t in range(t_new):
        acc = jnp.zeros(o_ref.shape[1:], F32) + cb_ref[...]
        for j in range(CONV_KERNEL):
            r = t + j
            tap = st_ref[r] if r < CONV_STATE else glu_ref[r - CONV_STATE]
            acc = acc + tap * w_ref[j:j + 1, :]
        o_ref[t] = _ln_silu(acc, lg_ref[...], lb_ref[...])


def _conv_sample(state_t, glu_t, conv_w, conv_b, ln_g, ln_b):
    t_new, bd, cw = glu_t.shape
    vec = lambda: _const_spec((1, cw))
    return pl.pallas_call(
        functools.partial(_conv_sample_kernel, t_new=t_new),
        grid=(1,),
        in_specs=[_const_spec(state_t.shape), _const_spec(glu_t.shape), _const_spec((CONV_KERNEL, cw)), vec(), vec(), vec()],
        out_specs=_const_spec((t_new, bd, cw)),
        out_shape=jax.ShapeDtypeStruct((t_new, bd, cw), F32),
        compiler_params=_cparams(("arbitrary",)),
        name="conv_sample",
    )(state_t, glu_t, conv_w, conv_b.reshape(1, cw), ln_g.reshape(1, cw), ln_b.reshape(1, cw))


def _merge_kernel(x_ref, a_ref, c_ref, ga_ref, gc_ref, wa_ref, wc_ref, o_ref):
    na = _rms(a_ref[...], ga_ref[...]).astype(BF16)
    nc = _rms(c_ref[...], gc_ref[...]).astype(BF16)
    y = jnp.dot(na, wa_ref[...], preferred_element_type=F32)
    y = y + jnp.dot(nc, wc_ref[...], preferred_element_type=F32)
    o_ref[...] = x_ref[...] + y


def _merge(x, attn, conv, g_a, g_c, w_out_bf, *, tm):
    nt, d = x.shape
    wa = attn.shape[1]
    wc = conv.shape[1]
    row = lambda width: pl.BlockSpec((tm, width), lambda i: (i, 0))
    return pl.pallas_call(
        _merge_kernel,
        grid=(nt // tm,),
        in_specs=[row(d), row(wa), row(wc), _const_spec((1, wa)), _const_spec((1, wc)),
                  pl.BlockSpec((wa, d), lambda i: (0, 0)), pl.BlockSpec((wc, d), lambda i: (wa // wc, 0))],
        out_specs=row(d),
        out_shape=jax.ShapeDtypeStruct((nt, d), F32),
        compiler_params=_cparams(("parallel",)),
        name="merge",
    )(x, attn, conv, g_a.reshape(1, wa), g_c.reshape(1, wc), w_out_bf, w_out_bf)


def _swiglu_core(h_bf, wg_ref, wu_ref, wd_ref):
    g = jnp.dot(h_bf, wg_ref[...], preferred_element_type=F32)
    u = jnp.dot(h_bf, wu_ref[...], preferred_element_type=F32)
    hid = (g * jax.nn.sigmoid(g) * u).astype(BF16)
    return jnp.dot(hid, wd_ref[...], preferred_element_type=F32)


def _ffn_dense_kernel(x_ref, g_ref, wg_ref, wu_ref, wd_ref, o_ref):
    x = x_ref[...]
    h = _rms(x, g_ref[...]).astype(BF16)
    o_ref[...] = x + _swiglu_core(h, wg_ref, wu_ref, wd_ref)


def _ffn_dense(x, g, wg_bf, wu_bf, wd_bf, *, tm):
    nt, d = x.shape
    f = wg_bf.shape[1]
    row = pl.BlockSpec((tm, d), lambda i: (i, 0))
    return pl.pallas_call(
        _ffn_dense_kernel,
        grid=(nt // tm,),
        in_specs=[row, _const_spec((1, d)), _const_spec((d, f)), _const_spec((d, f)), _const_spec((f, d))],
        out_specs=row,
        out_shape=jax.ShapeDtypeStruct((nt, d), F32),
        compiler_params=_cparams(("parallel",)),
        name="ffn_dense",
    )(x, g.reshape(1, d), wg_bf, wu_bf, wd_bf)


def _router_kernel(x_ref, g_ref, wr_ref, h_ref, idx_ref, gate_ref):
    h = _rms(x_ref[...], g_ref[...])
    h_hi, h_lo = _split_bf16(h)
    h_ref[...] = h_hi
    w = wr_ref[...]
    w_hi, w_lo = _split_bf16(w)
    logits = (jnp.dot(h_hi, w_hi, preferred_element_type=F32) + jnp.dot(h_hi, w_lo, preferred_element_type=F32)
              + jnp.dot(h_lo, w_hi, preferred_element_type=F32))
    lane = lax.broadcasted_iota(jnp.int32, logits.shape, 1)
    neg = jnp.float32(-jnp.inf)
    l1 = jnp.where(lane < N_EXPERTS, logits, neg)
    m1 = jnp.max(l1, axis=-1, keepdims=True)
    i1 = jnp.min(jnp.where(l1 == m1, lane, LANES), axis=-1, keepdims=True)
    l2 = jnp.where(lane == i1, neg, l1)
    m2 = jnp.max(l2, axis=-1, keepdims=True)
    i2 = jnp.min(jnp.where(l2 == m2, lane, LANES), axis=-1, keepdims=True)
    e = jnp.exp(m2 - m1)
    w1 = 1.0 / (1.0 + e)
    w2 = e / (1.0 + e)
    idx_ref[...] = jnp.where(lane == 0, i1, jnp.where(lane == 1, i2, 0))
    gate_ref[...] = jnp.where(lane == 0, w1, jnp.where(lane == 1, w2, 0.0))


def _router(x, g, w_router_pad, *, tm):
    nt, d = x.shape
    row = lambda width: pl.BlockSpec((tm, width), lambda i: (i, 0))
    return pl.pallas_call(
        _router_kernel,
        grid=(nt // tm,),
        in_specs=[row(d), _const_spec((1, d)), _const_spec((d, LANES))],
        out_specs=[row(d), row(LANES), row(LANES)],
        out_shape=[jax.ShapeDtypeStruct((nt, d), BF16), jax.ShapeDtypeStruct((nt, LANES), jnp.int32),
                   jax.ShapeDtypeStruct((nt, LANES), F32)],
        compiler_params=_cparams(("parallel",)),
        name="router",
    )(x, g.reshape(1, d), w_router_pad)


def _moe_experts_kernel(te_ref, h_ref, wg_ref, wu_ref, wd_ref, o_ref):
    t = pl.program_id(0)

    @pl.when(te_ref[t] < N_EXPERTS)
    def _():
        o_ref[...] = _swiglu_core(h_ref[...], wg_ref, wu_ref, wd_ref)

    @pl.when(te_ref[t] >= N_EXPERTS)
    def _():
        o_ref[...] = jnp.zeros_like(o_ref)


def _moe_experts(tile_expert, h_sorted, wg_bf, wu_bf, wd_bf, *, tmg):
    p, d = h_sorted.shape
    f = wg_bf.shape[2]
    ex = lambda t, te: (jnp.minimum(te[t], N_EXPERTS - 1), 0, 0)
    return pl.pallas_call(
        _moe_experts_kernel,
        grid_spec=pltpu.PrefetchScalarGridSpec(
            num_scalar_prefetch=1,
            grid=(p // tmg,),
            in_specs=[pl.BlockSpec((tmg, d), lambda t, te: (t, 0)),
                      pl.BlockSpec((None, d, f), ex), pl.BlockSpec((None, d, f), ex), pl.BlockSpec((None, f, d), ex)],
            out_specs=pl.BlockSpec((tmg, d), lambda t, te: (t, 0)),
        ),
        out_shape=jax.ShapeDtypeStruct((p, d), F32),
        compiler_params=_cparams(("arbitrary",)),
        name="moe_experts",
    )(tile_expert, h_sorted, wg_bf, wu_bf, wd_bf)


def _moe_combine_kernel(x_ref, y1_ref, y2_ref, gate_ref, o_ref):
    gate = gate_ref[...]
    o_ref[...] = x_ref[...] + gate[:, 0:1] * y1_ref[...] + gate[:, 1:2] * y2_ref[...]


def _moe_combine(x, y1, y2, gate, *, tm):
    nt, d = x.shape
    row = lambda width: pl.BlockSpec((tm, width), lambda i: (i, 0))
    return pl.pallas_call(
        _moe_combine_kernel,
        grid=(nt // tm,),
        in_specs=[row(d), row(d), row(d), row(LANES)],
        out_specs=row(d),
        out_shape=jax.ShapeDtypeStruct((nt, d), F32),
        compiler_params=_cparams(("parallel",)),
        name="moe_combine",
    )(x, y1, y2, gate)


def _moe_plan(idx, token_ok, *, tmg):
    nt = idx.shape[0]
    n_slots = nt * TOP_K
    p = (n_slots // tmg + N_EXPERTS) * tmg
    e_flat = jnp.where(token_ok[:, None], idx, N_EXPERTS).reshape(n_slots)
    order = jnp.argsort(e_flat, stable=True).astype(jnp.int32)
    e_sorted = e_flat[order]
    counts = jnp.sum(e_flat[None, :] == jnp.arange(N_EXPERTS + 1, dtype=jnp.int32)[:, None], axis=1).astype(jnp.int32)
    starts = jnp.cumsum(counts) - counts
    padded = ((counts + tmg - 1) // tmg) * tmg
    pstarts = jnp.cumsum(padded) - padded
    dest_sorted = jnp.arange(n_slots, dtype=jnp.int32) + (pstarts - starts)[e_sorted]
    dest_sorted = jnp.where(e_sorted < N_EXPERTS, dest_sorted, p)
    src_token = jnp.zeros((p,), jnp.int32).at[dest_sorted].set(order // TOP_K, mode="drop")
    dest = jnp.zeros((n_slots,), jnp.int32).at[order].set(jnp.minimum(dest_sorted, p - 1)).reshape(nt, TOP_K)
    tile_start = jnp.arange(p // tmg, dtype=jnp.int32) * tmg
    pends = pstarts[:N_EXPERTS] + padded[:N_EXPERTS]
    tile_expert = jnp.sum(tile_start[:, None] >= pends[None, :], axis=1).astype(jnp.int32)
    return src_token, tile_expert, dest


def _final_norm_kernel(x_ref, g_ref, o_ref):
    o_ref[...] = _rms(x_ref[...], g_ref[...])


def _final_norm(x, g, *, tm):
    nt, d = x.shape
    row = pl.BlockSpec((tm, d), lambda i: (i, 0))
    return pl.pallas_call(
        _final_norm_kernel,
        grid=(nt // tm,),
        in_specs=[row, _const_spec((1, d))],
        out_specs=row,
        out_shape=jax.ShapeDtypeStruct((nt, d), F32),
        compiler_params=_cparams(("parallel",)),
        name="final_norm",
    )(x, g.reshape(1, d))


def kernel(x_prompt, x_sample, cache_k, cache_v, state_conv, page_table, meta_tokens, g_mix, w_in, sb_bias, conv_w, conv_b, conv_ln_g, conv_ln_b, g_out_attn, g_out_conv, w_out, g_ffn, w_gate_dense, w_up_dense, w_down_dense, w_router, w_gate_moe, w_up_moe, w_down_moe, g_final):
    b, seq, d = x_prompt.shape
    bd, t_new, _ = x_sample.shape
    depth = g_mix.shape[0]
    cw = state_conv.shape[-1]
    n_real = N_META + seq
    seq_pad = -(-n_real // ATT_BLOCK) * ATT_BLOCK
    n_padrows = seq_pad - n_real
    assert n_padrows >= CONV_STATE, "the zero rows in front of each sequence must cover the conv prefix"
    n_prompt_rows = b * seq_pad
    n_sample_rows = bd * t_new
    nt = n_prompt_rows + n_sample_rows
    tm = 256
    assert nt % tm == 0 and seq_pad % ATT_BLOCK == 0

    xp = jnp.concatenate([jnp.zeros((b, n_padrows, d), F32), jnp.broadcast_to(meta_tokens[None], (b, N_META, d)), x_prompt], axis=1)
    x = jnp.concatenate([xp.reshape(n_prompt_rows, d), x_sample.reshape(n_sample_rows, d)], axis=0)
    row_id = jnp.arange(nt, dtype=jnp.int32)
    token_ok = (row_id >= n_prompt_rows) | (row_id % seq_pad >= n_padrows)

    tri = _suffix_tri(min(SUM_BLOCK, seq_pad))
    cache_kt = jnp.transpose(cache_k, (0, 1, 3, 4, 2))
    cache_vt = jnp.transpose(cache_v, (0, 1, 3, 4, 2))
    tmg = 256

    kp_l, vp_l, ks_l, vs_l, cp_l, cs_l = [], [], [], [], [], []
    for l in range(depth):
        q_bf, k_bf, v_bf, k_f, v_f, glu = _in_proj(x, g_mix[l], w_in[l].astype(BF16), tm=tm, n_batch=b,
                                                   seq_pad=seq_pad, n_padrows=n_padrows)
        bias2 = sb_bias[l] * LOG2E
        attn_p = _sb_prompt(q_bf, k_bf, v_bf, bias2, tri, n_batch=b, seq_pad=seq_pad)
        smp = lambda a: a[n_prompt_rows:].reshape(bd, t_new, a.shape[-1])
        bias_rows = jnp.tile(bias2, t_new)[:, None]
        attn_s = _sb_sample(page_table, smp(q_bf), smp(k_f), smp(v_f), cache_kt, cache_vt, bias_rows, tri, l)
        conv_p = _conv_prompt(glu, conv_w[l], conv_b[l], conv_ln_g[l], conv_ln_b[l], n_batch=b, seq_pad=seq_pad, tc=ATT_BLOCK)
        glu_s = smp(glu)
        conv_s = _conv_sample(state_conv[l].transpose(1, 0, 2), glu_s.transpose(1, 0, 2), conv_w[l], conv_b[l],
                              conv_ln_g[l], conv_ln_b[l]).transpose(1, 0, 2)
        attn = jnp.concatenate([attn_p, attn_s.reshape(n_sample_rows, SB_WIDTH)], axis=0)
        conv = jnp.concatenate([conv_p, conv_s.reshape(n_sample_rows, cw)], axis=0)
        x = _merge(x, attn, conv, g_out_attn[l], g_out_conv[l], w_out[l].astype(BF16), tm=tm)

        j = l // 2
        if l % 2 == 0:
            x = _ffn_dense(x, g_ffn[l], w_gate_dense[j].astype(BF16), w_up_dense[j].astype(BF16),
                           w_down_dense[j].astype(BF16), tm=tm)
        else:
            wr = jnp.zeros((d, LANES), F32).at[:, :N_EXPERTS].set(w_router[j])
            h_bf, idx, gate = _router(x, g_ffn[l], wr, tm=tm)
            src_token, tile_expert, dest = _moe_plan(idx[:, :TOP_K], token_ok, tmg=tmg)
            y = _moe_experts(tile_expert, h_bf[src_token], w_gate_moe[j].astype(BF16), w_up_moe[j].astype(BF16),
                             w_down_moe[j].astype(BF16), tmg=tmg)
            gate = jnp.where(token_ok[:, None], gate, 0.0)
            x = _moe_combine(x, y[dest[:, 0]], y[dest[:, 1]], gate, tm=tm)

        heads = lambda a, rows: a.reshape(rows + (SB_HEADS, SB_HEAD_DIM))
        kp_l.append(heads(k_f[:n_prompt_rows].reshape(b, seq_pad, SB_WIDTH)[:, n_padrows:], (b, n_real)))
        vp_l.append(heads(v_f[:n_prompt_rows].reshape(b, seq_pad, SB_WIDTH)[:, n_padrows:], (b, n_real)))
        ks_l.append(heads(smp(k_f), (bd, t_new)))
        vs_l.append(heads(smp(v_f), (bd, t_new)))
        cp_l.append(glu[:n_prompt_rows].reshape(b, seq_pad, cw)[:, seq_pad - CONV_STATE:])
        cs_l.append(jnp.concatenate([state_conv[l], glu_s], axis=1)[:, -CONV_STATE:])

    y = _final_norm(x, g_final, tm=tm)
    y_prompt = y[:n_prompt_rows].reshape(b, seq_pad, d)[:, seq_pad - seq:]
    y_sample = y[n_prompt_rows:].reshape(bd, t_new, d)
    return (y_prompt, y_sample, jnp.stack(kp_l), jnp.stack(vp_l), jnp.stack(ks_l), jnp.stack(vs_l),
            jnp.stack(cp_l), jnp.stack(cs_l))
```

```python
import functools

import jax
import jax.numpy as jnp
from jax import lax
from jax.experimental import pallas as pl
from jax.experimental.pallas import tpu as pltpu

F32 = jnp.float32
BF16 = jnp.bfloat16

N_META = 16
SB_HEADS = 8
SB_HEAD_DIM = 64
SB_WIDTH = SB_HEADS * SB_HEAD_DIM
CONV_KERNEL = 31
CONV_STATE = CONV_KERNEL - 1
N_EXPERTS = 8
TOP_K = 2
RMS_EPS = 1e-6
LN_EPS = 1e-5
LOG2E = 1.4426950408889634

LANES = 128
ATT_BLOCK = 128
SUM_BLOCK = 256
CONV_HALO = 32
VMEM_LIMIT = 56 * 1024 * 1024


def _cparams(sem):
    return pltpu.CompilerParams(dimension_semantics=sem, vmem_limit_bytes=VMEM_LIMIT)


def _rms(x, g):
    return x * lax.rsqrt(jnp.mean(x * x, axis=-1, keepdims=True) + RMS_EPS) * g


def _const_spec(shape):
    nd = len(shape)
    return pl.BlockSpec(shape, lambda *_: (0,) * nd)


def _in_proj_kernel(x_ref, g_ref, w_ref, qb_ref, kb_ref, vb_ref, kf_ref, vf_ref, glu_ref,
                    *, tm, n_batch, seq_pad, n_padrows):
    i = pl.program_id(0)
    h = _rms(x_ref[...], g_ref[...])
    r = i * tm + lax.broadcasted_iota(jnp.int32, (tm, 1), 0)
    rb = r
    for b in range(1, n_batch):
        rb = jnp.where(r >= b * seq_pad, r - b * seq_pad, rb)
    valid = (r >= n_batch * seq_pad) | (rb >= n_padrows)
    h = jnp.where(valid, h, 0.0).astype(BF16)
    w = SB_WIDTH
    q = jnp.dot(h, w_ref[:, 0:w], preferred_element_type=F32)
    qb_ref[...] = (q * (LOG2E * SB_HEAD_DIM ** -0.5)).astype(BF16)
    k = jnp.dot(h, w_ref[:, w:2 * w], preferred_element_type=F32)
    kf_ref[...] = k
    kb_ref[...] = k.astype(BF16)
    v = jnp.dot(h, w_ref[:, 2 * w:3 * w], preferred_element_type=F32)
    vf_ref[...] = v
    vb_ref[...] = v.astype(BF16)
    cw = (w_ref.shape[1] - 3 * w) // 2
    a = jnp.dot(h, w_ref[:, 3 * w:3 * w + cw], preferred_element_type=F32)
    g = jnp.dot(h, w_ref[:, 3 * w + cw:3 * w + 2 * cw], preferred_element_type=F32)
    glu_ref[...] = a * jax.nn.sigmoid(g)


def _in_proj(x, g, w_bf, *, tm, n_batch, seq_pad, n_padrows):
    nt, d = x.shape
    e = w_bf.shape[1]
    cw = (e - 3 * SB_WIDTH) // 2
    row = lambda width: pl.BlockSpec((tm, width), lambda i: (i, 0))
    return pl.pallas_call(
        functools.partial(_in_proj_kernel, tm=tm, n_batch=n_batch, seq_pad=seq_pad, n_padrows=n_padrows),
        grid=(nt // tm,),
        in_specs=[row(d), _const_spec((1, d)), _const_spec((d, e))],
        out_specs=[row(SB_WIDTH)] * 5 + [row(cw)],
        out_shape=[jax.ShapeDtypeStruct((nt, SB_WIDTH), BF16)] * 3
        + [jax.ShapeDtypeStruct((nt, SB_WIDTH), F32)] * 2 + [jax.ShapeDtypeStruct((nt, cw), F32)],
        compiler_params=_cparams(("parallel",)),
        name="in_proj",
    )(x, g.reshape(1, d), w_bf)


def _softplus2(z):
    return jnp.maximum(z, 0.0) + jnp.log2(1.0 + jnp.exp2(-jnp.abs(z)))


def _split_bf16(x):
    hi = x.astype(BF16)
    lo = (x - hi.astype(F32)).astype(BF16)
    return hi, lo


def _suffix_tri(n):
    j = lax.broadcasted_iota(jnp.int32, (n, n), 0)
    s = lax.broadcasted_iota(jnp.int32, (n, n), 1)
    return jnp.where(j >= s, 1.0, 0.0).astype(BF16)


def _sb_weights(z_blocks, valid_blocks, tri, run):
    m = z_blocks[0].shape[0]
    parts = []
    for z, valid in zip(z_blocks, valid_blocks):
        sp = _softplus2(z)
        if valid is not None:
            sp = jnp.where(valid, sp, 0.0)
        parts.append(sp.astype(BF16))
    cs = jnp.dot(jnp.concatenate(parts, axis=0), tri, preferred_element_type=F32)
    a_parts = [None] * len(z_blocks)
    for c in reversed(range(len(z_blocks))):
        blk = cs[c * m:(c + 1) * m]
        a = jnp.exp2(z_blocks[c] - blk - run)
        if valid_blocks[c] is not None:
            a = jnp.where(valid_blocks[c], a, 0.0)
        a_parts[c] = a.astype(BF16)
        run = run + blk[:, 0:1]
    return jnp.concatenate(a_parts, axis=1), run


def _sb_sweep(z_blocks, valid_blocks, v_rows, tri, run):
    a, run = _sb_weights(z_blocks, valid_blocks, tri, run)
    return jnp.dot(a, v_rows, preferred_element_type=F32), run


def _sb_prompt_kernel(bias_ref, q_ref, k_ref, v_ref, tri_ref, o_ref, qs_ref, acc_ref, *, span):
    hp = pl.program_id(1)
    qb = pl.program_id(2)
    n = ATT_BLOCK
    m = 2 * n
    sb = min(SUM_BLOCK, span)
    lane = lax.broadcasted_iota(jnp.int32, (1, LANES), 1)
    q = q_ref[...].astype(F32)
    qs_ref[0:n, :] = jnp.where(lane < SB_HEAD_DIM, q, 0.0).astype(BF16)
    qs_ref[n:m, :] = jnp.where(lane >= SB_HEAD_DIM, q, 0.0).astype(BF16)
    srow = lax.broadcasted_iota(jnp.int32, (m, 1), 0)
    bias = jnp.where(srow < n, bias_ref[hp * 2], bias_ref[hp * 2 + 1])
    q_pos = qb * n + jnp.where(srow < n, srow, srow - n)
    key_lane = lax.broadcasted_iota(jnp.int32, (1, sb), 1)
    hi_end = (qb + 1) * n
    n_sweeps = (hi_end + span - 1) // span

    def start_of(t):
        return pl.multiple_of(jnp.maximum(hi_end - (t + 1) * span, 0), n)

    def logits(t):
        return lax.dot_general(qs_ref[...], k_ref[pl.ds(start_of(t), span), :], (((1,), (1,)), ((), ())),
                               preferred_element_type=F32)

    def weights(z, t, limit, run):
        start = start_of(t)
        zb = [z[:, c * sb:(c + 1) * sb] + bias for c in range(span // sb)]
        vb = [None if limit is None else (start + c * sb + key_lane) < limit for c in range(span // sb)]
        return _sb_weights(zb, vb, tri_ref[...], run)

    def values(a, t):
        return jnp.dot(a, v_ref[pl.ds(start_of(t), span), :], preferred_element_type=F32)

    z0 = logits(0)
    z1 = logits(1)
    a0, run0 = weights(z0, 0, q_pos, jnp.zeros((m, 1), F32))
    acc_ref[...] = jnp.zeros_like(acc_ref)

    def step(t, carry):
        z_cur, a_prev, run = carry
        z_next = logits(t + 1)
        acc_ref[...] += values(a_prev, t - 1)
        a_cur, run = weights(z_cur, t, None, run)
        return z_next, a_cur, run

    n_mid = jnp.maximum(n_sweeps - 2, 0)
    carry = lax.fori_loop(0, n_mid // 2, lambda u, c: step(2 * u + 2, step(2 * u + 1, c)), (z1, a0, run0))
    z_last, a_prev, run = lax.fori_loop(0, n_mid % 2, lambda u, c: step(n_mid, c), carry)

    @pl.when(n_sweeps < 2)
    def _():
        acc_ref[...] += values(a_prev, 0)

    @pl.when(n_sweeps >= 2)
    def _():
        last = n_sweeps - 1
        done_from = hi_end - last * span
        acc_ref[...] += values(a_prev, last - 1)
        a_last, _ = weights(z_last, last, jnp.full((m, 1), done_from, jnp.int32), run)
        acc_ref[...] += values(a_last, last)

    o_ref[...] = jnp.where(lane < SB_HEAD_DIM, acc_ref[0:n, :], acc_ref[n:m, :])


def _sb_prompt(q_bf, k_bf, v_bf, bias2, tri, *, n_batch, seq_pad):
    n = ATT_BLOCK
    nqb = seq_pad // n
    n_hp = SB_WIDTH // LANES
    span = min(2 * SUM_BLOCK, seq_pad)
    sb = min(SUM_BLOCK, span)
    assert span % sb == 0 and tri.shape == (sb, sb)
    return pl.pallas_call(
        functools.partial(_sb_prompt_kernel, span=span),
        grid_spec=pltpu.PrefetchScalarGridSpec(
            num_scalar_prefetch=1,
            grid=(n_batch, n_hp, nqb),
            in_specs=[
                pl.BlockSpec((n, LANES), lambda b, h, i, *_: (b * nqb + i, h)),
                pl.BlockSpec((seq_pad, LANES), lambda b, h, i, *_: (b, h)),
                pl.BlockSpec((seq_pad, LANES), lambda b, h, i, *_: (b, h)),
                pl.BlockSpec((sb, sb), lambda b, h, i, *_: (0, 0)),
            ],
            out_specs=pl.BlockSpec((n, LANES), lambda b, h, i, *_: (b * nqb + i, h)),
            scratch_shapes=[pltpu.VMEM((2 * n, LANES), BF16), pltpu.VMEM((2 * n, LANES), F32)],
        ),
        out_shape=jax.ShapeDtypeStruct((n_batch * seq_pad, SB_WIDTH), F32),
        compiler_params=_cparams(("parallel", "parallel", "arbitrary")),
        name="sb_prompt",
    )(bias2, q_bf, k_bf, v_bf, tri)


def _sb_sample_kernel(*refs, t_new, pages_per_step):
    pg = pages_per_step
    pt_ref, bias_ref, q_ref, kn_ref, vn_ref = refs[:5]
    kt_refs = refs[5:5 + pg]
    vt_refs = refs[5 + pg:5 + 2 * pg]
    tri_ref, o_ref, qbd_ref, carry_ref, acc_ref = refs[5 + 2 * pg:]
    del pt_ref
    g = pl.program_id(1)
    sb = SUM_BLOCK
    m = t_new * SB_HEADS
    page = kt_refs[0].shape[2]
    bias = bias_ref[...]
    head_of_lane = lax.broadcasted_iota(jnp.int32, (SB_HEADS, SB_WIDTH), 1) // SB_HEAD_DIM
    head_of_row = lax.broadcasted_iota(jnp.int32, (SB_HEADS, SB_WIDTH), 0)
    nt_dims = (((1,), (1,)), ((), ()))

    @pl.when(g == 0)
    def _():
        q = q_ref[0].astype(F32)
        rows = []
        for t in range(t_new):
            qt = jnp.broadcast_to(q[t:t + 1, :], (SB_HEADS, SB_WIDTH))
            rows.append(jnp.where(head_of_lane == head_of_row, qt, 0.0))
        qbd = jnp.concatenate(rows, axis=0).astype(BF16)
        qbd_ref[...] = qbd
        pad = jnp.zeros((sb - t_new, SB_WIDTH), F32)
        k_new = jnp.concatenate([kn_ref[0], pad], axis=0).astype(BF16)
        v_new = jnp.concatenate([vn_ref[0], pad], axis=0).astype(BF16)
        z = lax.dot_general(qbd, k_new, nt_dims, preferred_element_type=F32) + bias
        s_pos = lax.broadcasted_iota(jnp.int32, (m, sb), 1)
        t_row = lax.broadcasted_iota(jnp.int32, (m, sb), 0) // SB_HEADS
        out, run = _sb_sweep([z], [s_pos < t_row], v_new, tri_ref[...], jnp.zeros((m, 1), F32))
        acc_ref[...] = out
        carry_ref[...] = run

    qbd = qbd_ref[...]
    zs = [jnp.dot(qbd, kt_ref[...].reshape(SB_WIDTH, page).astype(BF16), preferred_element_type=F32)
          for kt_ref in kt_refs]
    z = jnp.concatenate(zs, axis=1) + bias
    n_blocks = pg * page // sb
    a, run = _sb_weights([z[:, c * sb:(c + 1) * sb] for c in range(n_blocks)], [None] * n_blocks, tri_ref[...],
                         carry_ref[...])
    acc = acc_ref[...]
    for i, vt_ref in enumerate(vt_refs):
        vt = vt_ref[...].reshape(SB_WIDTH, page).astype(BF16)
        acc = acc + lax.dot_general(a[:, i * page:(i + 1) * page], vt, nt_dims, preferred_element_type=F32)
    carry_ref[...] = run
    acc_ref[...] = acc

    @pl.when(g == pl.num_programs(1) - 1)
    def _():
        for t in range(t_new):
            blk = acc_ref[t * SB_HEADS:(t + 1) * SB_HEADS, :]
            o_ref[0, t:t + 1, :] = jnp.sum(jnp.where(head_of_lane == head_of_row, blk, 0.0), axis=0, keepdims=True)


def _sb_sample(page_table, q_bf, k_new, v_new, cache_kt, cache_vt, bias_rows, tri, layer):
    bd, t_new, w = q_bf.shape
    m = t_new * SB_HEADS
    n_pages = page_table.shape[1]
    page = cache_kt.shape[4]
    pg = max(p for p in range(1, min(n_pages, 16) + 1) if n_pages % p == 0)
    assert (pg * page) % SUM_BLOCK == 0 and t_new <= SUM_BLOCK and tri.shape == (SUM_BLOCK, SUM_BLOCK)
    seq = lambda: pl.BlockSpec((1, t_new, w), lambda b, g, pt: (b, 0, 0))

    def cache(i):
        return pl.BlockSpec((None, None, SB_HEADS, SB_HEAD_DIM, page),
                            lambda b, g, pt: (layer, pt[b, n_pages - (g + 1) * pg + i], 0, 0, 0))

    return pl.pallas_call(
        functools.partial(_sb_sample_kernel, t_new=t_new, pages_per_step=pg),
        grid_spec=pltpu.PrefetchScalarGridSpec(
            num_scalar_prefetch=1,
            grid=(bd, n_pages // pg),
            in_specs=[pl.BlockSpec((m, 1), lambda b, g, pt: (0, 0)), seq(), seq(), seq()]
            + [cache(i) for i in range(pg)] + [cache(i) for i in range(pg)]
            + [pl.BlockSpec((SUM_BLOCK, SUM_BLOCK), lambda b, g, pt: (0, 0))],
            out_specs=seq(),
            scratch_shapes=[pltpu.VMEM((m, w), BF16), pltpu.VMEM((m, 1), F32), pltpu.VMEM((m, w), F32)],
        ),
        out_shape=jax.ShapeDtypeStruct((bd, t_new, w), F32),
        compiler_params=_cparams(("parallel", "arbitrary")),
        name="sb_sample",
    )(page_table, bias_rows, q_bf, k_new, v_new, *([cache_kt] * pg), *([cache_vt] * pg), tri)


def _ln_silu(y, g, b):
    mu = jnp.mean(y, axis=-1, keepdims=True)
    d = y - mu
    var = jnp.mean(d * d, axis=-1, keepdims=True)
    y = d * lax.rsqrt(var + LN_EPS) * g + b
    return y * jax.nn.sigmoid(y)


def _conv_prompt_kernel(glu_ref, w_ref, cb_ref, lg_ref, lb_ref, o_ref, win_ref, shift_ref, *, tc):
    i = pl.program_id(1)
    halo = CONV_HALO

    @pl.when(i == 0)
    def _():
        win_ref[0:halo, :] = jnp.zeros((halo, win_ref.shape[1]), F32)

    win_ref[halo:halo + tc, :] = glu_ref[...]
    acc = jnp.zeros((tc, win_ref.shape[1]), F32) + cb_ref[...]
    sub = 8
    for r in range(sub):
        offs = [halo - CONV_STATE + j for j in range(CONV_KERNEL) if (halo - CONV_STATE + j) % sub == r]
        if not offs:
            continue
        rows = max(offs) - r + tc
        shift_ref[0:rows, :] = win_ref[r:r + rows, :]
        for off in offs:
            j = off - (halo - CONV_STATE)
            acc = acc + shift_ref[off - r:off - r + tc, :] * w_ref[j:j + 1, :]
    o_ref[...] = _ln_silu(acc, lg_ref[...], lb_ref[...])
    win_ref[0:halo, :] = win_ref[tc:tc + halo, :]


def _conv_prompt(glu, conv_w, conv_b, ln_g, ln_b, *, n_batch, seq_pad, tc):
    cw = glu.shape[1]
    nb = seq_pad // tc
    vec = lambda: _const_spec((1, cw))
    return pl.pallas_call(
        functools.partial(_conv_prompt_kernel, tc=tc),
        grid=(n_batch, nb),
        in_specs=[pl.BlockSpec((tc, cw), lambda b, i: (b * nb + i, 0)), _const_spec((CONV_KERNEL, cw)), vec(), vec(), vec()],
        out_specs=pl.BlockSpec((tc, cw), lambda b, i: (b * nb + i, 0)),
        out_shape=jax.ShapeDtypeStruct((n_batch * seq_pad, cw), F32),
        scratch_shapes=[pltpu.VMEM((CONV_HALO + tc, cw), F32), pltpu.VMEM((CONV_HALO + tc, cw), F32)],
        compiler_params=_cparams(("parallel", "arbitrary")),
        name="conv_prompt",
    )(glu, conv_w, conv_b.reshape(1, cw), ln_g.reshape(1, cw), ln_b.reshape(1, cw))


def _conv_sample_kernel(st_ref, glu_ref, w_ref, cb_ref, lg_ref, lb_ref, o_ref, *, t_new):
    for t in range(t_new):
        acc = jnp.zeros(o_ref.shape[1:], F32) + cb_ref[...]
        for j in range(CONV_KERNEL):
            r = t + j
            tap = st_ref[r] if r < CONV_STATE else glu_ref[r - CONV_STATE]
            acc = acc + tap * w_ref[j:j + 1, :]
        o_ref[t] = _ln_silu(acc, lg_ref[...], lb_ref[...])


def _conv_sample(state_t, glu_t, conv_w, conv_b, ln_g, ln_b):
    t_new, bd, cw = glu_t.shape
    vec = lambda: _const_spec((1, cw))
    return pl.pallas_call(
        functools.partial(_conv_sample_kernel, t_new=t_new),
        grid=(1,),
        in_specs=[_const_spec(state_t.shape), _const_spec(glu_t.shape), _const_spec((CONV_KERNEL, cw)), vec(), vec(), vec()],
        out_specs=_const_spec((t_new, bd, cw)),
        out_shape=jax.ShapeDtypeStruct((t_new, bd, cw), F32),
        compiler_params=_cparams(("arbitrary",)),
        name="conv_sample",
    )(state_t, glu_t, conv_w, conv_b.reshape(1, cw), ln_g.reshape(1, cw), ln_b.reshape(1, cw))


def _merge_kernel(x_ref, a_ref, c_ref, ga_ref, gc_ref, wa_ref, wc_ref, o_ref):
    na = _rms(a_ref[...], ga_ref[...]).astype(BF16)
    nc = _rms(c_ref[...], gc_ref[...]).astype(BF16)
    y = jnp.dot(na, wa_ref[...], preferred_element_type=F32)
    y = y + jnp.dot(nc, wc_ref[...], preferred_element_type=F32)
    o_ref[...] = x_ref[...] + y


def _merge(x, attn, conv, g_a, g_c, w_out_bf, *, tm):
    nt, d = x.shape
    wa = attn.shape[1]
    wc = conv.shape[1]
    row = lambda width: pl.BlockSpec((tm, width), lambda i: (i, 0))
    return pl.pallas_call(
        _merge_kernel,
        grid=(nt // tm,),
        in_specs=[row(d), row(wa), row(wc), _const_spec((1, wa)), _const_spec((1, wc)),
                  pl.BlockSpec((wa, d), lambda i: (0, 0)), pl.BlockSpec((wc, d), lambda i: (wa // wc, 0))],
        out_specs=row(d),
        out_shape=jax.ShapeDtypeStruct((nt, d), F32),
        compiler_params=_cparams(("parallel",)),
        name="merge",
    )(x, attn, conv, g_a.reshape(1, wa), g_c.reshape(1, wc), w_out_bf, w_out_bf)


def _swiglu_core(h_bf, wg_ref, wu_ref, wd_ref):
    g = jnp.dot(h_bf, wg_ref[...], preferred_element_type=F32)
    u = jnp.dot(h_bf, wu_ref[...], preferred_element_type=F32)
    hid = (g * jax.nn.sigmoid(g) * u).astype(BF16)
    return jnp.dot(hid, wd_ref[...], preferred_element_type=F32)


def _ffn_dense_kernel(x_ref, g_ref, wg_ref, wu_ref, wd_ref, o_ref):
    x = x_ref[...]
    h = _rms(x, g_ref[...]).astype(BF16)
    o_ref[...] = x + _swiglu_core(h, wg_ref, wu_ref, wd_ref)


def _ffn_dense(x, g, wg_bf, wu_bf, wd_bf, *, tm):
    nt, d = x.shape
    f = wg_bf.shape[1]
    row = pl.BlockSpec((tm, d), lambda i: (i, 0))
    return pl.pallas_call(
        _ffn_dense_kernel,
        grid=(nt // tm,),
        in_specs=[row, _const_spec((1, d)), _const_spec((d, f)), _const_spec((d, f)), _const_spec((f, d))],
        out_specs=row,
        out_shape=jax.ShapeDtypeStruct((nt, d), F32),
        compiler_params=_cparams(("parallel",)),
        name="ffn_dense",
    )(x, g.reshape(1, d), wg_bf, wu_bf, wd_bf)


def _router_kernel(x_ref, g_ref, wr_ref, h_ref, idx_ref, gate_ref):
    h = _rms(x_ref[...], g_ref[...])
    h_hi, h_lo = _split_bf16(h)
    h_ref[...] = h
    w = wr_ref[...]
    w_hi, w_lo = _split_bf16(w)
    logits = (jnp.dot(h_hi, w_hi, preferred_element_type=F32) + jnp.dot(h_hi, w_lo, preferred_element_type=F32)
              + jnp.dot(h_lo, w_hi, preferred_element_type=F32))
    lane = lax.broadcasted_iota(jnp.int32, logits.shape, 1)
    neg = jnp.float32(-jnp.inf)
    l1 = jnp.where(lane < N_EXPERTS, logits, neg)
    m1 = jnp.max(l1, axis=-1, keepdims=True)
    i1 = jnp.min(jnp.where(l1 == m1, lane, LANES), axis=-1, keepdims=True)
    l2 = jnp.where(lane == i1, neg, l1)
    m2 = jnp.max(l2, axis=-1, keepdims=True)
    i2 = jnp.min(jnp.where(l2 == m2, lane, LANES), axis=-1, keepdims=True)
    e = jnp.exp(m2 - m1)
    w1 = 1.0 / (1.0 + e)
    w2 = e / (1.0 + e)
    idx_ref[...] = jnp.where(lane == 0, i1, jnp.where(lane == 1, i2, 0))
    gate_ref[...] = jnp.where(lane == 0, w1, jnp.where(lane == 1, w2, 0.0))


def _router(x, g, w_router_pad, *, tm):
    nt, d = x.shape
    row = lambda width: pl.BlockSpec((tm, width), lambda i: (i, 0))
    return pl.pallas_call(
        _router_kernel,
        grid=(nt // tm,),
        in_specs=[row(d), _const_spec((1, d)), _const_spec((d, LANES))],
        out_specs=[row(d), row(LANES), row(LANES)],
        out_shape=[jax.ShapeDtypeStruct((nt, d), F32), jax.ShapeDtypeStruct((nt, LANES), jnp.int32),
                   jax.ShapeDtypeStruct((nt, LANES), F32)],
        compiler_params=_cparams(("parallel",)),
        name="router",
    )(x, g.reshape(1, d), w_router_pad)


def _moe_experts_kernel(te_ref, h_ref, wg_ref, wu_ref, wd_ref, o_ref):
    t = pl.program_id(0)

    @pl.when(te_ref[t] < N_EXPERTS)
    def _():
        o_ref[...] = _swiglu_core(h_ref[...].astype(BF16), wg_ref, wu_ref, wd_ref)

    @pl.when(te_ref[t] >= N_EXPERTS)
    def _():
        o_ref[...] = jnp.zeros_like(o_ref)


def _moe_experts(tile_expert, h_sorted, wg_bf, wu_bf, wd_bf, *, tmg):
    p, d = h_sorted.shape
    f = wg_bf.shape[2]
    ex = lambda t, te: (jnp.minimum(te[t], N_EXPERTS - 1), 0, 0)
    return pl.pallas_call(
        _moe_experts_kernel,
        grid_spec=pltpu.PrefetchScalarGridSpec(
            num_scalar_prefetch=1,
            grid=(p // tmg,),
            in_specs=[pl.BlockSpec((tmg, d), lambda t, te: (t, 0)),
                      pl.BlockSpec((None, d, f), ex), pl.BlockSpec((None, d, f), ex), pl.BlockSpec((None, f, d), ex)],
            out_specs=pl.BlockSpec((tmg, d), lambda t, te: (t, 0)),
        ),
        out_shape=jax.ShapeDtypeStruct((p, d), F32),
        compiler_params=_cparams(("arbitrary",)),
        name="moe_experts",
    )(tile_expert, h_sorted, wg_bf, wu_bf, wd_bf)


def _moe_combine_kernel(x_ref, y1_ref, y2_ref, gate_ref, o_ref):
    gate = gate_ref[...]
    o_ref[...] = x_ref[...] + gate[:, 0:1] * y1_ref[...] + gate[:, 1:2] * y2_ref[...]


def _moe_combine(x, y1, y2, gate, *, tm):
    nt, d = x.shape
    row = lambda width: pl.BlockSpec((tm, width), lambda i: (i, 0))
    return pl.pallas_call(
        _moe_combine_kernel,
        grid=(nt // tm,),
        in_specs=[row(d), row(d), row(d), row(LANES)],
        out_specs=row(d),
        out_shape=jax.ShapeDtypeStruct((nt, d), F32),
        compiler_params=_cparams(("parallel",)),
        name="moe_combine",
    )(x, y1, y2, gate)


def _moe_plan(idx, token_ok, *, tmg):
    nt = idx.shape[0]
    n_slots = nt * TOP_K
    p = (n_slots // tmg + N_EXPERTS) * tmg
    e_flat = jnp.where(token_ok[:, None], idx, N_EXPERTS).reshape(n_slots)
    order = jnp.argsort(e_flat, stable=True).astype(jnp.int32)
    rank = jnp.argsort(order).astype(jnp.int32)
    counts = jnp.sum(e_flat[None, :] == jnp.arange(N_EXPERTS + 1, dtype=jnp.int32)[:, None], axis=1).astype(jnp.int32)
    starts = jnp.cumsum(counts) - counts
    padded = ((counts + tmg - 1) // tmg) * tmg
    pstarts = jnp.cumsum(padded) - padded
    shift = pstarts - starts
    dest = jnp.minimum(rank + shift[e_flat], p - 1).reshape(nt, TOP_K)
    pends = (pstarts + padded)[:N_EXPERTS]
    tile_start = jnp.arange(p // tmg, dtype=jnp.int32) * tmg
    tile_expert = jnp.sum(tile_start[:, None] >= pends[None, :], axis=1).astype(jnp.int32)
    row = jnp.arange(p, dtype=jnp.int32)
    row_expert = jnp.repeat(tile_expert, tmg)
    in_group = (row - pstarts[row_expert]) < counts[row_expert]
    sorted_pos = jnp.clip(row - shift[row_expert], 0, n_slots - 1)
    src_token = jnp.where(in_group & (row_expert < N_EXPERTS), order[sorted_pos] // TOP_K, 0)
    return src_token, tile_expert, dest


def _final_norm_kernel(x_ref, g_ref, o_ref):
    o_ref[...] = _rms(x_ref[...], g_ref[...])


def _final_norm(x, g, *, tm):
    nt, d = x.shape
    row = pl.BlockSpec((tm, d), lambda i: (i, 0))
    return pl.pallas_call(
        _final_norm_kernel,
        grid=(nt // tm,),
        in_specs=[row, _const_spec((1, d))],
        out_specs=row,
        out_shape=jax.ShapeDtypeStruct((nt, d), F32),
        compiler_params=_cparams(("parallel",)),
        name="final_norm",
    )(x, g.reshape(1, d))


def kernel(x_prompt, x_sample, cache_k, cache_v, state_conv, page_table, meta_tokens, g_mix, w_in, sb_bias, conv_w, conv_b, conv_ln_g, conv_ln_b, g_out_attn, g_out_conv, w_out, g_ffn, w_gate_dense, w_up_dense, w_down_dense, w_router, w_gate_moe, w_up_moe, w_down_moe, g_final):
    b, seq, d = x_prompt.shape
    bd, t_new, _ = x_sample.shape
    depth = g_mix.shape[0]
    cw = state_conv.shape[-1]
    n_real = N_META + seq
    seq_pad = -(-n_real // ATT_BLOCK) * ATT_BLOCK
    n_padrows = seq_pad - n_real
    assert n_padrows >= CONV_STATE, "the zero rows in front of each sequence must cover the conv prefix"
    n_prompt_rows = b * seq_pad
    n_sample_rows = bd * t_new
    nt = n_prompt_rows + n_sample_rows
    tm = 256
    assert nt % tm == 0 and seq_pad % ATT_BLOCK == 0

    xp = jnp.concatenate([jnp.zeros((b, n_padrows, d), F32), jnp.broadcast_to(meta_tokens[None], (b, N_META, d)), x_prompt], axis=1)
    x = jnp.concatenate([xp.reshape(n_prompt_rows, d), x_sample.reshape(n_sample_rows, d)], axis=0)
    row_id = jnp.arange(nt, dtype=jnp.int32)
    token_ok = (row_id >= n_prompt_rows) | (row_id % seq_pad >= n_padrows)

    tri = _suffix_tri(min(SUM_BLOCK, seq_pad))
    cache_kt = jnp.transpose(cache_k, (0, 1, 3, 4, 2))
    cache_vt = jnp.transpose(cache_v, (0, 1, 3, 4, 2))
    tmg = 256

    kp_l, vp_l, ks_l, vs_l, cp_l, cs_l = [], [], [], [], [], []
    for l in range(depth):
        q_bf, k_bf, v_bf, k_f, v_f, glu = _in_proj(x, g_mix[l], w_in[l].astype(BF16), tm=tm, n_batch=b,
                                                   seq_pad=seq_pad, n_padrows=n_padrows)
        bias2 = sb_bias[l] * LOG2E
        attn_p = _sb_prompt(q_bf, k_bf, v_bf, bias2, tri, n_batch=b, seq_pad=seq_pad)
        smp = lambda a: a[n_prompt_rows:].reshape(bd, t_new, a.shape[-1])
        bias_rows = jnp.tile(bias2, t_new)[:, None]
        attn_s = _sb_sample(page_table, smp(q_bf), smp(k_f), smp(v_f), cache_kt, cache_vt, bias_rows, tri, l)
        conv_p = _conv_prompt(glu, conv_w[l], conv_b[l], conv_ln_g[l], conv_ln_b[l], n_batch=b, seq_pad=seq_pad, tc=ATT_BLOCK)
        glu_s = smp(glu)
        conv_s = _conv_sample(state_conv[l].transpose(1, 0, 2), glu_s.transpose(1, 0, 2), conv_w[l], conv_b[l],
                              conv_ln_g[l], conv_ln_b[l]).transpose(1, 0, 2)
        attn = jnp.concatenate([attn_p, attn_s.reshape(n_sample_rows, SB_WIDTH)], axis=0)
        conv = jnp.concatenate([conv_p, conv_s.reshape(n_sample_rows, cw)], axis=0)
        x = _merge(x, attn, conv, g_out_attn[l], g_out_conv[l], w_out[l].astype(BF16), tm=tm)

        j = l // 2
        if l % 2 == 0:
            x = _ffn_dense(x, g_ffn[l], w_gate_dense[j].astype(BF16), w_up_dense[j].astype(BF16),
                           w_down_dense[j].astype(BF16), tm=tm)
        else:
            wr = jnp.zeros((d, LANES), F32).at[:, :N_EXPERTS].set(w_router[j])
            h_n, idx, gate = _router(x, g_ffn[l], wr, tm=tm)
            src_token, tile_expert, dest = _moe_plan(idx[:, :TOP_K], token_ok, tmg=tmg)
            y = _moe_experts(tile_expert, h_n[src_token], w_gate_moe[j].astype(BF16), w_up_moe[j].astype(BF16),
                             w_down_moe[j].astype(BF16), tmg=tmg)
            gate = jnp.where(token_ok[:, None], gate, 0.0)
            x = _moe_combine(x, y[dest[:, 0]], y[dest[:, 1]], gate, tm=tm)

        heads = lambda a, rows: a.reshape(rows + (SB_HEADS, SB_HEAD_DIM))
        kp_l.append(heads(k_f[:n_prompt_rows].reshape(b, seq_pad, SB_WIDTH)[:, n_padrows:], (b, n_real)))
        vp_l.append(heads(v_f[:n_prompt_rows].reshape(b, seq_pad, SB_WIDTH)[:, n_padrows:], (b, n_real)))
        ks_l.append(heads(smp(k_f), (bd, t_new)))
        vs_l.append(heads(smp(v_f), (bd, t_new)))
        cp_l.append(glu[:n_prompt_rows].reshape(b, seq_pad, cw)[:, seq_pad - CONV_STATE:])
        cs_l.append(jnp.concatenate([state_conv[l], glu_s], axis=1)[:, -CONV_STATE:])

    y = _final_norm(x, g_final, tm=tm)
    y_prompt = y[:n_prompt_rows].reshape(b, seq_pad, d)[:, seq_pad - seq:]
    y_sample = y[n_prompt_rows:].reshape(bd, t_new, d)
    return (y_prompt, y_sample, jnp.stack(kp_l), jnp.stack(vp_l), jnp.stack(ks_l), jnp.stack(vs_l),
            jnp.stack(cp_l), jnp.stack(cs_l))
```
